```python
import jax
import jax.numpy as jnp
from jax import lax
import numpy as np

D_MODEL = 1024
BATCH = 32
SEQ = 256
DEPTH = 2
DEC_BATCH = 8
DEC_SEQ = 4096
PAST_LEN = 512

GRID_W = 64
N_EVEN = (DEPTH + 1) // 2
N_ODD = DEPTH // 2
EPS = 1e-6
Q_BLOCK = 128
ROPE_BASE = 10000.0

B_HEADS = 8
B_KV_HEADS = 2
B_HEAD_DIM = 64
A_HEADS = 4
A_QK_DIM = 64
A_V_DIM = 128
A_CHUNK = 64
EVEN_IN = B_HEADS * B_HEAD_DIM + 2 * B_KV_HEADS * B_HEAD_DIM + A_HEADS * (2 * A_QK_DIM + 2 * A_V_DIM) + 4 * A_HEADS
EVEN_MIX = B_HEADS * B_HEAD_DIM + A_HEADS * A_V_DIM
C_HEADS = 8
C_Q_LORA = 384
C_KV_LORA = 256
C_NOPE = 64
C_ROPE = 32
C_V = 64
D_RNN = 512
RG_BLOCKS = 8
RG_BW = D_RNN // RG_BLOCKS
RG_C = 8.0
CONV_W = 4
CONV_LEFT = 2
ODD_IN = C_Q_LORA + C_KV_LORA + C_ROPE + 2 * D_RNN
ODD_MIX = C_HEADS * C_V + D_RNN
N_EXPERTS = 16
EXPERT_FF = 512
EC_CAPACITY = 2

kernel_name = 'hybrid_diffusion_prefix_step'


def rms_norm(x, w):
    xf = x.astype(jnp.float32)
    y = xf * lax.rsqrt(jnp.mean(xf * xf, axis=-1, keepdims=True) + EPS)
    return (y * w.astype(jnp.float32)).astype(x.dtype)


def split_cols(a, sizes):
    return jnp.split(a, [int(s) for s in np.cumsum(sizes)[:-1]], axis=-1)


def grid_positions(n_tokens):
    rows = n_tokens // GRID_W
    row = jnp.repeat(jnp.arange(rows, dtype=jnp.float32), GRID_W)
    col = jnp.tile(jnp.arange(GRID_W, dtype=jnp.float32), rows)
    return row, col


def rope_axis(x, pos):
    half = x.shape[-1] // 2
    inv_freq = ROPE_BASE ** (-jnp.arange(half, dtype=jnp.float32) / half)
    ang = pos[:, None] * inv_freq[None, :]
    cos = jnp.cos(ang)[None, :, None, :]
    sin = jnp.sin(ang)[None, :, None, :]
    xf = x.astype(jnp.float32)
    x1, x2 = xf[..., :half], xf[..., half:]
    return jnp.concatenate([x1 * cos - x2 * sin, x1 * sin + x2 * cos], axis=-1).astype(x.dtype)


def rope_2d(x, row, col):
    d = x.shape[-1] // 2
    return jnp.concatenate([rope_axis(x[..., :d], row), rope_axis(x[..., d:], col)], axis=-1)


def attend(q, k, v, scale):
    s = jnp.einsum('bqhgd,bshd->bhgqs', q, k).astype(jnp.float32) * scale
    p = jax.nn.softmax(s, axis=-1).astype(v.dtype)
    return jnp.einsum('bhgqs,bshd->bqhgd', p, v)


def attend_blocked(q, k, v, scale):
    bsz, n_q = q.shape[:2]
    nb = n_q // Q_BLOCK
    qb = jnp.moveaxis(q.reshape((bsz, nb, Q_BLOCK) + q.shape[2:]), 1, 0)
    out = lax.map(lambda qi: attend(qi, k, v, scale), qb)
    return jnp.moveaxis(out, 0, 1).reshape((bsz, n_q) + out.shape[3:])


def mlstm_chunkwise(q, k, v, i_pre, log_f, c0, n0, m0):
    bsz, n_tok, n_heads, _ = q.shape
    nc = n_tok // A_CHUNK

    def to_chunks(a):
        a = a.astype(jnp.float32).reshape((bsz, nc, A_CHUNK) + a.shape[2:])
        return jnp.moveaxis(jnp.moveaxis(a, 1, 0), 3, 2)

    mask = jnp.tril(jnp.ones((A_CHUNK, A_CHUNK), dtype=bool))

    def step(carry, inp):
        cm, nv, m = carry
        qc, kc, vc, ic, fc = inp
        b = jnp.cumsum(fc, axis=-1)
        dmat = jnp.where(mask, b[..., :, None] - b[..., None, :] + ic[..., None, :], -jnp.inf)
        m_inter = b + m[..., None]
        m_t = jnp.maximum(m_inter, dmat.max(axis=-1))
        w_inter = jnp.exp(m_inter - m_t)
        s = jnp.einsum('bhtd,bhsd->bhts', qc, kc) * jnp.exp(dmat - m_t[..., None])
        num = w_inter[..., None] * jnp.einsum('bhtd,bhde->bhte', qc, cm) + jnp.einsum('bhts,bhse->bhte', s, vc)
        den = w_inter * jnp.einsum('bhtd,bhd->bht', qc, nv) + s.sum(axis=-1)
        h = num / jnp.maximum(jnp.abs(den), jnp.exp(-m_t))[..., None]
        b_last = b[..., -1]
        g = b_last[..., None] - b + ic
        m_new = jnp.maximum(b_last + m, g.max(axis=-1))
        decay = jnp.exp(b_last + m - m_new)
        wk = jnp.exp(g - m_new[..., None])
        c_new = decay[..., None, None] * cm + jnp.einsum('bhs,bhsd,bhse->bhde', wk, kc, vc)
        n_new = decay[..., None] * nv + jnp.einsum('bhs,bhsd->bhd', wk, kc)
        return (c_new, n_new, m_new), h

    init = (c0.astype(jnp.float32), n0.astype(jnp.float32), m0.astype(jnp.float32))
    (c_f, n_f, m_f), h = lax.scan(step, init, (to_chunks(q), to_chunks(k), to_chunks(v), to_chunks(i_pre), to_chunks(log_f)))
    h = jnp.moveaxis(jnp.moveaxis(h, 0, 1), 2, 3).reshape(bsz, n_tok, n_heads, -1)
    return h, c_f, n_f, m_f


def mlstm_bidir(q, k, v, gates, c0, n0, m0):
    flip = lambda a: jnp.flip(a, axis=1)
    h_f, cf, nf, mf = mlstm_chunkwise(q, k, v, gates[:, :, 0], jax.nn.log_sigmoid(gates[:, :, 1]), c0[:, 0], n0[:, 0], m0[:, 0])
    h_b, cb, nb, mb = mlstm_chunkwise(flip(q), flip(k), flip(v), flip(gates[:, :, 2]), flip(jax.nn.log_sigmoid(gates[:, :, 3])), c0[:, 1], n0[:, 1], m0[:, 1])
    h = h_f + flip(h_b)
    return h, jnp.stack([cf, cb], axis=1), jnp.stack([nf, nb], axis=1), jnp.stack([mf, mb], axis=1)


def conv_centred(x, w, b):
    y = lax.conv_general_dilated(x, w[:, None, :].astype(x.dtype), window_strides=(1,),
                                 padding=[(CONV_LEFT, CONV_W - 1 - CONV_LEFT)],
                                 dimension_numbers=('NWC', 'WIO', 'NWC'), feature_group_count=x.shape[-1])
    return y + b.astype(x.dtype)


def rglru_scan(x, wa, ba, wx, bx, lam, h0):
    bsz, n_tok, _ = x.shape
    xb = x.reshape(bsz, n_tok, RG_BLOCKS, RG_BW)
    r = jax.nn.sigmoid(jnp.einsum('btnd,nde->btne', xb, wa.astype(jnp.float32)).reshape(bsz, n_tok, D_RNN) + ba.astype(jnp.float32))
    i = jax.nn.sigmoid(jnp.einsum('btnd,nde->btne', xb, wx.astype(jnp.float32)).reshape(bsz, n_tok, D_RNN) + bx.astype(jnp.float32))
    log_a = -RG_C * r * jax.nn.softplus(-lam.astype(jnp.float32))
    a = jnp.exp(log_a)
    b = jnp.sqrt(-jnp.expm1(2.0 * log_a)) * (i * x)
    b = b.at[:, 0].add(a[:, 0] * h0.astype(jnp.float32))
    _, h = lax.associative_scan(lambda e1, e2: (e1[0] * e2[0], e2[0] * e1[1] + e2[1]), (a, b), axis=1)
    return h


def ec_moe(x, router_w, w_gate, w_up, w_down):
    bsz, n_tok, _ = x.shape
    cap = n_tok * EC_CAPACITY // N_EXPERTS
    aff = jax.nn.softmax(jnp.einsum('btd,de->bte', x, router_w).astype(jnp.float32), axis=-1)
    gate, idx = lax.top_k(jnp.swapaxes(aff, 1, 2), cap)
    bidx = jnp.arange(bsz)[:, None, None]
    xe = x[bidx, idx]
    hid = jax.nn.silu(jnp.einsum('becd,edf->becf', xe, w_gate)) * jnp.einsum('becd,edf->becf', xe, w_up)
    ye = jnp.einsum('becf,efd->becd', hid, w_down) * gate[..., None].astype(x.dtype)
    return jnp.zeros_like(x).at[bidx, idx].add(ye)


def adaln(cond, w, b):
    m = (jax.nn.silu(cond) @ w + b)[:, None, :]
    return jnp.split(m, 6, axis=-1)


def even_mixer(xm, j, P, ctx):
    bsz, n_tok, _ = xm.shape
    proj = xm @ P['ev_w_in'][j]
    qb, kb, vb, qa, ka, va, oa, ga = split_cols(proj, [B_HEADS * B_HEAD_DIM, B_KV_HEADS * B_HEAD_DIM, B_KV_HEADS * B_HEAD_DIM,
                                                      A_HEADS * A_QK_DIM, A_HEADS * A_QK_DIM, A_HEADS * A_V_DIM, A_HEADS * A_V_DIM, 4 * A_HEADS])
    qb = rms_norm(qb.reshape(bsz, n_tok, B_HEADS, B_HEAD_DIM), P['gqa_q_norm'][j])
    kb = rms_norm(kb.reshape(bsz, n_tok, B_KV_HEADS, B_HEAD_DIM), P['gqa_k_norm'][j])
    vb = vb.reshape(bsz, n_tok, B_KV_HEADS, B_HEAD_DIM)
    if ctx is None:
        k_all, v_all = kb, vb
        c0 = jnp.zeros((bsz, 2, A_HEADS, A_QK_DIM, A_V_DIM), jnp.float32)
        n0 = jnp.zeros((bsz, 2, A_HEADS, A_QK_DIM), jnp.float32)
        m0 = jnp.zeros((bsz, 2, A_HEADS), jnp.float32)
    else:
        c0, n0, m0, ctx_k, ctx_v = ctx
        row, col = grid_positions(n_tok)
        qb = rope_2d(qb, row, col)
        k_all = jnp.concatenate([ctx_k.astype(xm.dtype), rope_2d(kb, row, col)], axis=1)
        v_all = jnp.concatenate([ctx_v.astype(xm.dtype), vb], axis=1)
    ob = attend_blocked(qb.reshape(bsz, n_tok, B_KV_HEADS, B_HEADS // B_KV_HEADS, B_HEAD_DIM), k_all, v_all, B_HEAD_DIM ** -0.5)
    ob = ob.reshape(bsz, n_tok, B_HEADS * B_HEAD_DIM)
    gates = (ga + P['mlstm_gate_b'][j]).astype(jnp.float32).reshape(bsz, n_tok, 4, A_HEADS)
    qa = qa.reshape(bsz, n_tok, A_HEADS, A_QK_DIM)
    ka = ka.reshape(bsz, n_tok, A_HEADS, A_QK_DIM) * (A_QK_DIM ** -0.5)
    va = va.reshape(bsz, n_tok, A_HEADS, A_V_DIM)
    ha, c_f, n_f, m_f = mlstm_bidir(qa, ka, va, gates, c0, n0, m0)
    ha = rms_norm(ha.astype(xm.dtype), P['mlstm_norm_w'][j].reshape(A_HEADS, A_V_DIM)).reshape(bsz, n_tok, A_HEADS * A_V_DIM)
    ha = ha * jax.nn.sigmoid(oa)
    out = jnp.concatenate([ob, ha], axis=-1) @ P['ev_w_out'][j]
    st = (c_f, n_f, m_f, kb, vb) if ctx is None else None
    return out, st


def odd_mixer(xm, j, P, ctx):
    bsz, n_tok, _ = xm.shape
    proj = xm @ P['od_w_in'][j]
    cq, ckv, kr, xr, xg = split_cols(proj, [C_Q_LORA, C_KV_LORA, C_ROPE, D_RNN, D_RNN])
    cq = rms_norm(cq, P['mla_q_norm'][j])
    q = (cq @ P['mla_w_uq'][j]).reshape(bsz, n_tok, C_HEADS, C_NOPE + C_ROPE)
    q_nope, q_rope = q[..., :C_NOPE], q[..., C_NOPE:]
    ckv = rms_norm(ckv, P['mla_kv_norm'][j])
    if ctx is None:
        ckv_all, kr_all = ckv, kr
        h0 = jnp.zeros((bsz, 2, D_RNN), jnp.float32)
    else:
        ctx_ckv, ctx_kr, h0 = ctx
        row, col = grid_positions(n_tok)
        q_rope = rope_2d(q_rope, row, col)
        kr_rot = rope_2d(kr[:, :, None, :], row, col)[:, :, 0]
        ckv_all = jnp.concatenate([ctx_ckv.astype(xm.dtype), ckv], axis=1)
        kr_all = jnp.concatenate([ctx_kr.astype(xm.dtype), kr_rot], axis=1)
    n_keys = ckv_all.shape[1]
    kv = (ckv_all @ P['mla_w_ukv'][j]).reshape(bsz, n_keys, C_HEADS, C_NOPE + C_V)
    k = jnp.concatenate([kv[..., :C_NOPE], jnp.broadcast_to(kr_all[:, :, None, :], (bsz, n_keys, C_HEADS, C_ROPE))], axis=-1)
    v = kv[..., C_NOPE:]
    qf = jnp.concatenate([q_nope, q_rope], axis=-1)[:, :, :, None, :]
    oc = attend_blocked(qf, k, v, (C_NOPE + C_ROPE) ** -0.5).reshape(bsz, n_tok, C_HEADS * C_V)
    xc = conv_centred(xr, P['rg_conv_w'][j], P['rg_conv_b'][j]).astype(jnp.float32)
    hf = rglru_scan(xc, P['rg_wa'][j, 0], P['rg_ba'][j, 0], P['rg_wx'][j, 0], P['rg_bx'][j, 0], P['rg_lambda'][j, 0], h0[:, 0])
    hb = rglru_scan(jnp.flip(xc, axis=1), P['rg_wa'][j, 1], P['rg_ba'][j, 1], P['rg_wx'][j, 1], P['rg_bx'][j, 1], P['rg_lambda'][j, 1], h0[:, 1])
    rg = (hf + jnp.flip(hb, axis=1)).astype(xm.dtype) * jax.nn.gelu(xg)
    out = jnp.concatenate([oc, rg], axis=-1) @ P['od_w_out'][j]
    st = (ckv, kr, jnp.stack([hf[:, -1], hb[:, -1]], axis=1)) if ctx is None else None
    return out, st


def trunk(x, cond, P, caches):
    ev_states, od_states = [], []
    for layer in range(DEPTH):
        j = layer // 2
        sh1, sc1, g1, sh2, sc2, g2 = adaln(cond, P['ada_w'][layer], P['ada_b'][layer])
        xm = rms_norm(x, P['norm1_w'][layer]) * (1 + sc1) + sh1
        if layer % 2 == 0:
            ctx = None if caches is None else (caches['mC'][:, j], caches['mn'][:, j], caches['mm'][:, j], caches['gk'][:, j], caches['gv'][:, j])
            out, st = even_mixer(xm, j, P, ctx)
            if st is not None:
                ev_states.append(st)
        else:
            ctx = None if caches is None else (caches['ckv'][:, j], caches['kr'][:, j], caches['rh'][:, j])
            out, st = odd_mixer(xm, j, P, ctx)
            if st is not None:
                od_states.append(st)
        x = x + g1 * out
        xm = rms_norm(x, P['norm2_w'][layer]) * (1 + sc2) + sh2
        x = x + g2 * ec_moe(xm, P['router_w'][layer], P['exp_w_gate'][layer], P['exp_w_up'][layer], P['exp_w_down'][layer])
    return rms_norm(x, P['final_norm_w']), ev_states, od_states


def setup_inputs(seed: int = 0) -> dict:
    key = jax.random.key(seed)
    ks = jax.random.split(key, 40)
    f32 = jnp.float32

    def nrm(k, shape, scale=1.0):
        return jax.random.normal(k, shape, f32) * scale

    def gain(k, shape):
        return 1.0 + nrm(k, shape, 0.05)

    gate_base = jnp.repeat(jnp.array([0.0, 3.0, 0.0, 3.0], f32), A_HEADS)
    return {
        'x_prompt': nrm(ks[0], (BATCH, SEQ, D_MODEL)),
        'x_sample': nrm(ks[1], (DEC_BATCH, DEC_SEQ, D_MODEL)),
        'state_mlstm_C': nrm(ks[2], (DEC_BATCH, N_EVEN, 2, A_HEADS, A_QK_DIM, A_V_DIM), 0.1),
        'state_mlstm_n': nrm(ks[3], (DEC_BATCH, N_EVEN, 2, A_HEADS, A_QK_DIM), 0.1),
        'state_mlstm_m': nrm(ks[4], (DEC_BATCH, N_EVEN, 2, A_HEADS)),
        'cache_gqa_k': nrm(ks[5], (DEC_BATCH, N_EVEN, PAST_LEN, B_KV_HEADS, B_HEAD_DIM)),
        'cache_gqa_v': nrm(ks[6], (DEC_BATCH, N_EVEN, PAST_LEN, B_KV_HEADS, B_HEAD_DIM)),
        'cache_mla_ckv': nrm(ks[7], (DEC_BATCH, N_ODD, PAST_LEN, C_KV_LORA)),
        'cache_mla_krope': nrm(ks[8], (DEC_BATCH, N_ODD, PAST_LEN, C_ROPE)),
        'state_rglru_h': nrm(ks[9], (DEC_BATCH, N_ODD, 2, D_RNN), 0.5),
        'c': nrm(ks[10], (DEC_BATCH, D_MODEL)),
        'c_ctx': nrm(ks[11], (D_MODEL,)),
        'norm1_w': gain(ks[12], (DEPTH, D_MODEL)),
        'norm2_w': gain(ks[13], (DEPTH, D_MODEL)),
        'final_norm_w': gain(ks[14], (D_MODEL,)),
        'ada_w': nrm(ks[15], (DEPTH, D_MODEL, 6 * D_MODEL), D_MODEL ** -0.5),
        'ada_b': nrm(ks[16], (DEPTH, 6 * D_MODEL), 0.02),
        'ev_w_in': nrm(ks[17], (N_EVEN, D_MODEL, EVEN_IN), D_MODEL ** -0.5),
        'ev_w_out': nrm(ks[18], (N_EVEN, EVEN_MIX, D_MODEL), EVEN_MIX ** -0.5),
        'gqa_q_norm': gain(ks[19], (N_EVEN, B_HEAD_DIM)),
        'gqa_k_norm': gain(ks[20], (N_EVEN, B_HEAD_DIM)),
        'mlstm_gate_b': gate_base[None, :] + nrm(ks[21], (N_EVEN, 4 * A_HEADS), 0.1),
        'mlstm_norm_w': gain(ks[22], (N_EVEN, A_HEADS * A_V_DIM)),
        'od_w_in': nrm(ks[23], (N_ODD, D_MODEL, ODD_IN), D_MODEL ** -0.5),
        'od_w_out': nrm(ks[24], (N_ODD, ODD_MIX, D_MODEL), ODD_MIX ** -0.5),
        'mla_q_norm': gain(ks[25], (N_ODD, C_Q_LORA)),
        'mla_kv_norm': gain(ks[26], (N_ODD, C_KV_LORA)),
        'mla_w_uq': nrm(ks[27], (N_ODD, C_Q_LORA, C_HEADS * (C_NOPE + C_ROPE)), C_Q_LORA ** -0.5),
        'mla_w_ukv': nrm(ks[28], (N_ODD, C_KV_LORA, C_HEADS * (C_NOPE + C_V)), C_KV_LORA ** -0.5),
        'rg_conv_w': nrm(ks[29], (N_ODD, CONV_W, D_RNN), CONV_W ** -0.5),
        'rg_conv_b': nrm(ks[30], (N_ODD, D_RNN), 0.02),
        'rg_wa': nrm(ks[31], (N_ODD, 2, RG_BLOCKS, RG_BW, RG_BW), RG_BW ** -0.5),
        'rg_ba': nrm(ks[32], (N_ODD, 2, D_RNN), 0.02),
        'rg_wx': nrm(ks[33], (N_ODD, 2, RG_BLOCKS, RG_BW, RG_BW), RG_BW ** -0.5),
        'rg_bx': nrm(ks[34], (N_ODD, 2, D_RNN), 0.02),
        'rg_lambda': 4.3 + 4.7 * jax.random.uniform(ks[35], (N_ODD, 2, D_RNN), f32),
        'router_w': nrm(ks[36], (DEPTH, D_MODEL, N_EXPERTS), D_MODEL ** -0.5),
        'exp_w_gate': nrm(ks[37], (DEPTH, N_EXPERTS, D_MODEL, EXPERT_FF), D_MODEL ** -0.5),
        'exp_w_up': nrm(ks[38], (DEPTH, N_EXPERTS, D_MODEL, EXPERT_FF), D_MODEL ** -0.5),
        'exp_w_down': nrm(ks[39], (DEPTH, N_EXPERTS, EXPERT_FF, D_MODEL), EXPERT_FF ** -0.5),
    }


def reference(x_prompt, x_sample, state_mlstm_C, state_mlstm_n, state_mlstm_m, cache_gqa_k, cache_gqa_v,
              cache_mla_ckv, cache_mla_krope, state_rglru_h, c, c_ctx, norm1_w, norm2_w, final_norm_w,
              ada_w, ada_b, ev_w_in, ev_w_out, gqa_q_norm, gqa_k_norm, mlstm_gate_b, mlstm_norm_w,
              od_w_in, od_w_out, mla_q_norm, mla_kv_norm, mla_w_uq, mla_w_ukv, rg_conv_w, rg_conv_b,
              rg_wa, rg_ba, rg_wx, rg_bx, rg_lambda, router_w, exp_w_gate, exp_w_up, exp_w_down):
    P = dict(norm1_w=norm1_w, norm2_w=norm2_w, final_norm_w=final_norm_w, ada_w=ada_w, ada_b=ada_b,
             ev_w_in=ev_w_in, ev_w_out=ev_w_out, gqa_q_norm=gqa_q_norm, gqa_k_norm=gqa_k_norm,
             mlstm_gate_b=mlstm_gate_b, mlstm_norm_w=mlstm_norm_w, od_w_in=od_w_in, od_w_out=od_w_out,
             mla_q_norm=mla_q_norm, mla_kv_norm=mla_kv_norm, mla_w_uq=mla_w_uq, mla_w_ukv=mla_w_ukv,
             rg_conv_w=rg_conv_w, rg_conv_b=rg_conv_b, rg_wa=rg_wa, rg_ba=rg_ba, rg_wx=rg_wx, rg_bx=rg_bx,
             rg_lambda=rg_lambda, router_w=router_w, exp_w_gate=exp_w_gate, exp_w_up=exp_w_up, exp_w_down=exp_w_down)
    y_prompt, ev_st, od_st = trunk(x_prompt, c_ctx[None, :], P, None)
    dt = x_prompt.dtype
    new_mC = jnp.stack([s[0] for s in ev_st], axis=1).astype(dt)
    new_mn = jnp.stack([s[1] for s in ev_st], axis=1).astype(dt)
    new_mm = jnp.stack([s[2] for s in ev_st], axis=1).astype(dt)
    new_gk = jnp.stack([s[3] for s in ev_st], axis=1).astype(dt)
    new_gv = jnp.stack([s[4] for s in ev_st], axis=1).astype(dt)
    new_ckv = jnp.stack([s[0] for s in od_st], axis=1).astype(dt)
    new_kr = jnp.stack([s[1] for s in od_st], axis=1).astype(dt)
    new_rh = jnp.stack([s[2] for s in od_st], axis=1).astype(dt)
    caches = dict(mC=state_mlstm_C, mn=state_mlstm_n, mm=state_mlstm_m, gk=cache_gqa_k, gv=cache_gqa_v,
                  ckv=cache_mla_ckv, kr=cache_mla_krope, rh=state_rglru_h)
    y_sample, _, _ = trunk(x_sample, c, P, caches)
    return (y_prompt, y_sample, new_mC, new_mn, new_mm, new_gk, new_gv, new_ckv, new_kr, new_rh)
```

```python
import functools

import numpy as np
import jax
import jax.numpy as jnp
from jax import lax
from jax.experimental import pallas as pl
from jax.experimental.pallas import tpu as pltpu

F32 = jnp.float32
BF16 = jnp.bfloat16

D_MODEL = 1024
N_SAMPLE_GROUPS = 8
N_CTX_GROUPS = 2
N_GROUPS = N_SAMPLE_GROUPS + N_CTX_GROUPS
GROUP_TOKENS = 4096
SEG = 256
SEGS_PER_GROUP = GROUP_TOKENS // SEG
PAST_LEN = 512
GRID_W = 64
EPS = 1e-6
ROPE_BASE = 10000.0

B_HEADS, B_KV_HEADS, B_HEAD_DIM = 8, 2, 64
A_HEADS, A_QK_DIM, A_V_DIM = 4, 64, 128
C_HEADS, C_Q_LORA, C_KV_LORA, C_NOPE, C_ROPE, C_V = 8, 384, 256, 64, 32, 64
C_QPAD = 128
D_RNN, RG_BLOCKS, RG_C, CONV_W, CONV_LEFT = 512, 8, 8.0, 4, 2
N_EXPERTS, EXPERT_FF = 16, 512
SLOTS = 512
CTX_CAP = 2 * SEG // N_EXPERTS
LANES = 128
TOKEN_TILE = 512
VMEM_LIMIT = 56 * 1024 * 1024


def _cparams(n_axes, vmem=VMEM_LIMIT):
    return pltpu.CompilerParams(dimension_semantics=("arbitrary",) * n_axes, vmem_limit_bytes=vmem)


def _dot(a, b):
    return jnp.dot(a, b, preferred_element_type=F32)


def _dot_nt(a, b):
    return lax.dot_general(a, b, (((1,), (1,)), ((), ())), preferred_element_type=F32)


def _split3(a):
    hi = a.astype(BF16)
    r = a - hi.astype(F32)
    mid = r.astype(BF16)
    lo = (r - mid.astype(F32)).astype(BF16)
    return hi, mid, lo


def _dot_x3(a, w_hi, w_lo):
    a_hi = a.astype(BF16)
    a_lo = (a - a_hi.astype(F32)).astype(BF16)
    return _dot(a_hi, w_hi) + (_dot(a_lo, w_hi) + _dot(a_hi, w_lo))


def _rms(x, w):
    return (x * lax.rsqrt(jnp.mean(x * x, axis=-1, keepdims=True) + EPS)) * w


def _sigmoid(x):
    return 1.0 / (1.0 + jnp.exp(-x))


def _rope(x, cos, sin_signed, half):
    n = x.shape[-1]
    lane = lax.broadcasted_iota(jnp.int32, x.shape, x.ndim - 1)
    first = (lane % (2 * half)) < half
    partner = jnp.where(first, pltpu.roll(x, n - half, x.ndim - 1), pltpu.roll(x, half, x.ndim - 1))
    return x * cos + partner * sin_signed


def _tile_lanes(a, reps):
    return a if reps == 1 else jnp.concatenate([a] * reps, axis=-1)


def _adaln_kernel(c_ref, w_ref, b_ref, o_ref):
    c = c_ref[...]
    a = c * _sigmoid(c)
    w = w_ref[0]
    w_hi = w.astype(BF16)
    w_lo = (w - w_hi.astype(F32)).astype(BF16)
    o_ref[0] = _dot_x3(a, w_hi, w_lo) + b_ref[0]


def _adaln(cond, ada_w, ada_b):
    depth, _, n6 = ada_w.shape
    rows = cond.shape[0]
    tn = 1024
    return pl.pallas_call(
        _adaln_kernel,
        grid=(depth, n6 // tn),
        in_specs=[
            pl.BlockSpec((rows, D_MODEL), lambda l, j: (0, 0)),
            pl.BlockSpec((1, D_MODEL, tn), lambda l, j: (l, 0, j)),
            pl.BlockSpec((1, 1, tn), lambda l, j: (l, 0, j)),
        ],
        out_specs=pl.BlockSpec((1, rows, tn), lambda l, j: (l, 0, j)),
        out_shape=jax.ShapeDtypeStruct((depth, rows, n6), F32),
        compiler_params=_cparams(2),
        name="adaln",
    )(cond, ada_w, ada_b.reshape(depth, 1, n6))


def _grp_spec(tm, width):
    return pl.BlockSpec((1, tm, width), lambda g, i: (g, i, 0))


def _mod_spec():
    return pl.BlockSpec((1, 1, D_MODEL), lambda g, i: (g, 0, 0))


def _const_spec(shape):
    nd = len(shape)
    return pl.BlockSpec(shape, lambda g, i: (0,) * nd)


def _two_source_specs(tm):
    last = GROUP_TOKENS // tm - 1
    xs = pl.BlockSpec((1, tm, D_MODEL), lambda g, i: (jnp.minimum(g, N_SAMPLE_GROUPS - 1),
                                                      jnp.where(g < N_SAMPLE_GROUPS, i, last), 0))
    xp = pl.BlockSpec((1, tm, D_MODEL), lambda g, i: (jnp.maximum(g - N_SAMPLE_GROUPS, 0),
                                                      jnp.where(g < N_SAMPLE_GROUPS, 0, i), 0))
    return xs, xp


def _rope_spec(tm):
    return pl.BlockSpec((1, tm, LANES), lambda g, i: (jnp.where(g < N_SAMPLE_GROUPS, 0, 1), i, 0))


def _rope_tables(d):
    half = d // 4
    t = np.arange(GROUP_TOKENS)
    pos = np.stack([(t // GRID_W).astype(np.float32), (t % GRID_W).astype(np.float32)], axis=1)
    lane = np.arange(d)
    axis = lane // (d // 2)
    j = lane % half
    first = (lane % (d // 2)) < half
    inv_freq = (np.float32(ROPE_BASE) ** (-(np.arange(half, dtype=np.float32)) / np.float32(half))).astype(np.float32)
    ang = (pos[:, axis] * inv_freq[j][None, :]).astype(np.float32)
    cos = np.cos(ang.astype(np.float64)).astype(np.float32)
    sin = np.sin(ang.astype(np.float64)).astype(np.float32)
    sin = np.where(first[None, :], -sin, sin)
    return cos, sin


def _pad_tables(cos, sin):
    t, w = cos.shape
    cos_p = np.ones((t, LANES), np.float32)
    sin_p = np.zeros((t, LANES), np.float32)
    cos_p[:, :w] = cos
    sin_p[:, :w] = sin
    cos2 = np.stack([cos_p, np.ones_like(cos_p)])
    sin2 = np.stack([sin_p, np.zeros_like(sin_p)])
    return jnp.asarray(cos2), jnp.asarray(sin2)


EV_COLS = dict(q=(0, 512), k=(512, 640), v=(640, 768), qa=(768, 1024), ka=(1024, 1280), va=(1280, 1792),
               oa=(1792, 2304))
EV_MAIN = 2304


LOG2E = 1.4426950408889634
V_SLOT = LANES


def _in_even_kernel(xs_ref, xp_ref, nw_ref, sc_ref, sh_ref, w_ref, wgh_ref, wgl_ref, gb_ref, bd_ref, qnw_ref,
                    knw_ref, cos_ref, sin_ref, wv_ref, vone_ref,
                    q_ref, kb_ref, vb_ref, qa_ref, ka_ref, va_ref, oa_ref, ga_ref, k0_ref, k1_ref, vaug_ref):
    g = pl.program_id(0)
    x = jnp.where(g >= N_SAMPLE_GROUPS, xp_ref[0], xs_ref[0])
    xm = _rms(x, nw_ref[...]) * (1.0 + sc_ref[0]) + sh_ref[0]
    xb = xm.astype(BF16)
    cos = cos_ref[0]
    sin = sin_ref[0]

    def proj(name):
        lo, hi = EV_COLS[name]
        return _dot(xb, w_ref[:, lo:hi])

    q = proj("q")
    q = q * lax.rsqrt(_dot((q * q).astype(BF16), bd_ref[...]) + EPS) * qnw_ref[...]
    q = _rope(q, _tile_lanes(cos, 4), _tile_lanes(sin, 4), B_HEAD_DIM // 4)
    q_ref[0] = (q * (B_HEAD_DIM ** -0.5 * LOG2E)).astype(BF16)
    k = proj("k")
    k = k * lax.rsqrt(_dot((k * k).astype(BF16), bd_ref[0:LANES, 0:LANES]) + EPS) * knw_ref[...]
    k = _rope(k, cos, sin, B_HEAD_DIM // 4)
    kb_ref[0] = k
    k0_ref[0] = k[:, :B_HEAD_DIM].astype(BF16)
    k1_ref[0] = k[:, B_HEAD_DIM:].astype(BF16)
    vb_ref[0] = proj("v")
    vaug_ref[0] = (_dot(xb, wv_ref[...]) + vone_ref[...]).astype(BF16)
    qa_ref[0] = proj("qa").astype(BF16)
    ka_ref[0] = (proj("ka") * (A_QK_DIM ** -0.5)).astype(BF16)
    va_ref[0] = proj("va").astype(BF16)
    oa_ref[0] = proj("oa").astype(BF16)
    ga_ref[0] = _dot_x3(xm, wgh_ref[...], wgl_ref[...]) + gb_ref[...]


def _in_even(xs, xp, nw, sc, sh, w_main, wg_hi, wg_lo, gb, bd, qnw, knw, cos, sin, w_vaug, v_one):
    tm = TOKEN_TILE
    xs_spec, xp_spec = _two_source_specs(tm)
    widths = [(512, BF16), (128, F32), (128, F32), (256, BF16), (256, BF16), (512, BF16), (512, BF16), (128, F32),
              (B_HEAD_DIM, BF16), (B_HEAD_DIM, BF16), (B_KV_HEADS * V_SLOT, BF16)]
    return pl.pallas_call(
        _in_even_kernel,
        grid=(N_GROUPS, GROUP_TOKENS // tm),
        in_specs=[xs_spec, xp_spec, _const_spec((1, D_MODEL)), _mod_spec(), _mod_spec(),
                  _const_spec(w_main.shape), _const_spec(wg_hi.shape), _const_spec(wg_lo.shape),
                  _const_spec(gb.shape), _const_spec(bd.shape), _const_spec(qnw.shape), _const_spec(knw.shape),
                  _rope_spec(tm), _rope_spec(tm), _const_spec(w_vaug.shape), _const_spec(v_one.shape)],
        out_specs=[_grp_spec(tm, w) for w, _ in widths],
        out_shape=[jax.ShapeDtypeStruct((N_GROUPS, GROUP_TOKENS, w), dt) for w, dt in widths],
        compiler_params=_cparams(2),
        name="in_even",
    )(xs, xp, nw, sc, sh, w_main, wg_hi, wg_lo, gb, bd, qnw, knw, cos, sin, w_vaug, v_one)


def _softmax_pv(q, keys, vals, dv):
    scores = [_dot_nt(q, k) for k in keys]
    m = functools.reduce(jnp.maximum, [jnp.max(s, axis=-1, keepdims=True) for s in scores])
    acc = None
    for s, v in zip(scores, vals):
        pv = _dot(jnp.exp2(s - m).astype(BF16), v)
        acc = pv if acc is None else acc + pv
    return acc[:, :dv] / acc[:, dv:dv + 1]


def _attn_branches(g, emit):
    @pl.when(g < N_SAMPLE_GROUPS)
    def _():
        emit(True)

    @pl.when(g >= N_SAMPLE_GROUPS)
    def _():
        emit(False)


def _gqa_kernel(q_ref, k0_ref, k1_ref, v_ref, kc_ref, vc_ref, o_ref):
    c = pl.program_id(1)
    own = pl.ds(pl.multiple_of(c * SEG, SEG), SEG)
    q = q_ref[0]
    pair = 2 * B_HEAD_DIM

    def emit(latent):
        outs = []
        for kv, k_ref in enumerate((k0_ref, k1_ref)):
            vl = slice(kv * V_SLOT, (kv + 1) * V_SLOT)
            if latent:
                keys, vals = [kc_ref[0, kv], k_ref[0]], [vc_ref[0, :, vl], v_ref[0, :, vl]]
            else:
                keys, vals = [k_ref[0, own, :]], [v_ref[0, own, vl]]
            for half in range(B_HEADS // B_KV_HEADS // 2):
                lo = (kv * B_HEADS // B_KV_HEADS + 2 * half) * B_HEAD_DIM
                qg = jnp.concatenate([q[:, lo:lo + B_HEAD_DIM], q[:, lo + B_HEAD_DIM:lo + pair]], axis=0)
                o = _softmax_pv(qg, keys, vals, B_HEAD_DIM)
                outs += [o[:SEG], o[SEG:]]
        o_ref[0] = jnp.concatenate(outs, axis=-1).astype(BF16)

    _attn_branches(pl.program_id(0), emit)


def _gqa_attention(q, k0, k1, v_aug, kc, vc):
    cache = lambda nd: (lambda g, c: (jnp.minimum(g, N_SAMPLE_GROUPS - 1),) + (0,) * (nd - 1))
    whole = lambda w: pl.BlockSpec((1, GROUP_TOKENS, w), lambda g, c: (g, 0, 0))
    return pl.pallas_call(
        _gqa_kernel,
        grid=(N_GROUPS, SEGS_PER_GROUP),
        in_specs=[pl.BlockSpec((1, SEG, B_HEADS * B_HEAD_DIM), lambda g, c: (g, c, 0)),
                  whole(B_HEAD_DIM), whole(B_HEAD_DIM), whole(B_KV_HEADS * V_SLOT),
                  pl.BlockSpec((1, B_KV_HEADS, PAST_LEN, B_HEAD_DIM), cache(4)),
                  pl.BlockSpec((1, PAST_LEN, B_KV_HEADS * V_SLOT), cache(3))],
        out_specs=pl.BlockSpec((1, SEG, B_HEADS * B_HEAD_DIM), lambda g, c: (g, c, 0)),
        out_shape=jax.ShapeDtypeStruct((N_GROUPS, GROUP_TOKENS, B_HEADS * B_HEAD_DIM), BF16),
        compiler_params=_cparams(2),
        name="gqa_attention",
    )(q, k0, k1, v_aug, kc, vc)


MLA_PAIR = 2


def _mla_kernel(q_ref, k_ref, v_ref, kc_ref, vc_ref, o_ref):
    c = pl.program_id(2)
    own = pl.ds(pl.multiple_of(c * SEG, SEG), SEG)
    q = q_ref[0]

    def emit(latent):
        outs = []
        for j in range(MLA_PAIR):
            kl = slice(j * C_QPAD, (j + 1) * C_QPAD)
            vl = slice(j * V_SLOT, (j + 1) * V_SLOT)
            if latent:
                keys, vals = [kc_ref[0, :, kl], k_ref[0, :, kl]], [vc_ref[0, :, vl], v_ref[0, :, vl]]
            else:
                keys, vals = [k_ref[0, own, kl]], [v_ref[0, own, vl]]
            outs.append(_softmax_pv(q[:, kl], keys, vals, C_V))
        o_ref[0] = jnp.concatenate(outs, axis=-1).astype(BF16)

    _attn_branches(pl.program_id(0), emit)


def _mla_attention(q, k, v_aug, kc, vc):
    wq, wv = MLA_PAIR * C_QPAD, MLA_PAIR * V_SLOT
    cache = lambda g, p, c: (jnp.minimum(g, N_SAMPLE_GROUPS - 1), 0, p)
    return pl.pallas_call(
        _mla_kernel,
        grid=(N_GROUPS, C_HEADS // MLA_PAIR, SEGS_PER_GROUP),
        in_specs=[pl.BlockSpec((1, SEG, wq), lambda g, p, c: (g, c, p)),
                  pl.BlockSpec((1, GROUP_TOKENS, wq), lambda g, p, c: (g, 0, p)),
                  pl.BlockSpec((1, GROUP_TOKENS, wv), lambda g, p, c: (g, 0, p)),
                  pl.BlockSpec((1, PAST_LEN, wq), cache),
                  pl.BlockSpec((1, PAST_LEN, wv), cache)],
        out_specs=pl.BlockSpec((1, SEG, MLA_PAIR * C_V), lambda g, p, c: (g, c, p)),
        out_shape=jax.ShapeDtypeStruct((N_GROUPS, GROUP_TOKENS, C_HEADS * C_V), BF16),
        compiler_params=_cparams(3),
        name="mla_attention",
    )(q, k, v_aug, kc, vc)


def _out_kernel(*refs, even):
    if even:
        (xs_ref, xp_ref, att_ref, h_ref, gate_ref, g1_ref, w_ref, mnw_ref, n2w_ref, sc_ref, sh_ref, rwh_ref,
         rwl_ref, x1_ref, xpk_ref, aff_ref) = refs
        x = jnp.where(pl.program_id(0) >= N_SAMPLE_GROUPS, xp_ref[0], xs_ref[0])
        h = h_ref[0]
        mnw = mnw_ref[...]
        heads = [_rms(h[:, j * A_V_DIM:(j + 1) * A_V_DIM], mnw[:, j * A_V_DIM:(j + 1) * A_V_DIM])
                 for j in range(A_HEADS)]
        mix = jnp.concatenate(heads, axis=-1) * _sigmoid(gate_ref[0].astype(F32))
    else:
        (x_ref, att_ref, h_ref, gate_ref, g1_ref, w_ref, n2w_ref, sc_ref, sh_ref, rwh_ref, rwl_ref,
         x1_ref, xpk_ref, aff_ref) = refs
        x = x_ref[0]
        mix = h_ref[0] * jax.nn.gelu(gate_ref[0].astype(F32))
    half = w_ref.shape[0] // 2
    out = _dot(att_ref[0], w_ref[0:half, :]) + _dot(mix.astype(BF16), w_ref[half:, :])
    x1 = x + g1_ref[0] * out
    x1_ref[0] = x1
    xm = _rms(x1, n2w_ref[...]) * (1.0 + sc_ref[0]) + sh_ref[0]
    xpk_ref[0] = xm
    logits = _dot_x3(xm, rwh_ref[...], rwl_ref[...])
    lane = lax.broadcasted_iota(jnp.int32, logits.shape, 1)
    logits = jnp.where(lane < N_EXPERTS, logits, -jnp.inf)
    e = jnp.exp(logits - jnp.max(logits, axis=-1, keepdims=True))
    aff_ref[0] = e / jnp.sum(e, axis=-1, keepdims=True)


def _out_proj(x_args, att, h, gate, g1, w, mnw, n2w, sc, sh, rw_hi, rw_lo, *, even):
    tm = TOKEN_TILE
    if even:
        x_specs = list(_two_source_specs(tm))
        extra, extra_specs = [mnw], [_const_spec(mnw.shape)]
    else:
        x_specs = [_grp_spec(tm, D_MODEL)]
        extra, extra_specs = [], []
    widths = [(D_MODEL, F32), (D_MODEL, F32), (LANES, F32)]
    return pl.pallas_call(
        functools.partial(_out_kernel, even=even),
        grid=(N_GROUPS, GROUP_TOKENS // tm),
        in_specs=x_specs + [_grp_spec(tm, 512), _grp_spec(tm, 512), _grp_spec(tm, 512), _mod_spec(),
                            _const_spec(w.shape)] + extra_specs +
                 [_const_spec(n2w.shape), _mod_spec(), _mod_spec(), _const_spec(rw_hi.shape),
                  _const_spec(rw_lo.shape)],
        out_specs=[_grp_spec(tm, wd) for wd, _ in widths],
        out_shape=[jax.ShapeDtypeStruct((N_GROUPS, GROUP_TOKENS, wd), dt) for wd, dt in widths],
        compiler_params=_cparams(2),
        name="out_even" if even else "out_odd",
    )(*x_args, att, h, gate, g1, w, *extra, n2w, sc, sh, rw_hi, rw_lo)


def _scan_flags():
    g = pl.program_id(0)
    c = pl.program_id(1)
    cb = SEGS_PER_GROUP - 1 - c
    per_seq = jnp.where(g >= N_SAMPLE_GROUPS, 1, SEGS_PER_GROUP)
    starts = ((c % per_seq) == 0, (cb % per_seq) == per_seq - 1)
    ends = ((c % per_seq) == per_seq - 1, (cb % per_seq) == 0)
    return c, cb, starts, ends


def _store_or_add(ref, chunk, value, first_touch):
    rows = pl.ds(pl.multiple_of(chunk * SEG, SEG), SEG)

    @pl.when(first_touch)
    def _():
        ref[0, rows, :] = value

    @pl.when(jnp.logical_not(first_touch))
    def _():
        ref[0, rows, :] += value


def _log_sigmoid(x):
    return jnp.minimum(x, 0.0) - jnp.log1p(jnp.exp(-jnp.abs(x)))


C_AUG = 2 * A_V_DIM


def _mlstm_kernel(qf_ref, kf_ref, vf_ref, gf_ref, qb_ref, kb_ref, vb_ref, gb_ref, s0_ref, m0_ref,
                  h_ref, sf_ref, sb_ref, mf_ref, mb_ref, c_scr, m_scr):
    c, cb, starts, _ = _scan_flags()
    L = SEG
    row = lax.broadcasted_iota(jnp.int32, (L, L), 0)
    col = lax.broadcasted_iota(jnp.int32, (L, L), 1)
    lane = lax.broadcasted_iota(jnp.int32, (1, LANES), 1)
    ones_col = (lax.broadcasted_iota(jnp.int32, (L, A_V_DIM), 1) == 0).astype(BF16)

    def run(d, q_ref, k_ref, v_ref, g_ref, chunk, start, first_touch):
        @pl.when(start)
        def _():
            c_scr[d] = s0_ref[0, d]
            m_scr[d] = m0_ref[0, d]

        mask = (col <= row) if d == 0 else (col >= row)
        tri = mask.astype(BF16)
        gates = g_ref[0]
        lf = _log_sigmoid(gates)
        hi, mid, lo = _split3(lf)
        cum = _dot(tri, hi) + (_dot(tri, mid) + _dot(tri, lo))
        total = jnp.sum(lf, axis=0, keepdims=True)
        gates_t = gates.T
        cum_t = cum.T
        q = q_ref[0]
        k = k_ref[0]
        v = v_ref[0]
        k_t = k.astype(F32).T
        m_vec = m_scr[d]
        m_out = m_vec
        hs = []
        for h in range(A_HEADS):
            ci, cf = d * 2 * A_HEADS + h, d * 2 * A_HEADS + A_HEADS + h
            bc, br = cum[:, cf:cf + 1], cum_t[cf:cf + 1, :]
            ir = gates_t[ci:ci + 1, :]
            m_prev = m_vec[:, h:h + 1]
            dmat = jnp.where(mask, bc - br + ir, -jnp.inf)
            m_inter = bc + m_prev
            m_t = jnp.maximum(m_inter, jnp.max(dmat, axis=-1, keepdims=True))
            w_inter = jnp.exp(m_inter - m_t)
            qh = q[:, h * A_QK_DIM:(h + 1) * A_QK_DIM]
            kh = k[:, h * A_QK_DIM:(h + 1) * A_QK_DIM]
            s = _dot_nt(qh, kh) * jnp.exp(dmat - m_t)
            state = c_scr[d, h]
            qc = _dot(qh, state.astype(BF16))
            v_aug = jnp.concatenate([v[:, h * A_V_DIM:(h + 1) * A_V_DIM], ones_col], axis=-1)
            sv = _dot(s.astype(BF16), v_aug)
            num = w_inter * qc[:, :A_V_DIM] + sv[:, :A_V_DIM]
            den = w_inter * qc[:, A_V_DIM:A_V_DIM + 1] + jnp.sum(s, axis=-1, keepdims=True)
            hs.append(num / jnp.maximum(jnp.abs(den), jnp.exp(-m_t)))
            tot = total[:, cf:cf + 1]
            g_row = tot - br + ir
            m_new = jnp.maximum(tot + m_prev, jnp.max(g_row, axis=-1, keepdims=True))
            decay = jnp.exp(tot + m_prev - m_new)
            kw = (k_t[h * A_QK_DIM:(h + 1) * A_QK_DIM, :] * jnp.exp(g_row - m_new)).astype(BF16)
            c_scr[d, h] = decay * state + _dot(kw, v_aug)
            m_out = jnp.where(lane == h, m_new, m_out)
        m_scr[d] = m_out
        _store_or_add(h_ref, chunk, jnp.concatenate(hs, axis=-1), first_touch)

    run(0, qf_ref, kf_ref, vf_ref, gf_ref, c, starts[0], c < cb)
    run(1, qb_ref, kb_ref, vb_ref, gb_ref, cb, starts[1], c < cb)
    sf_ref[0, 0] = c_scr[0]
    sb_ref[0, 0] = c_scr[1]
    mf_ref[0, 0] = m_scr[0]
    mb_ref[0, 0] = m_scr[1]


def _mlstm(qa, ka, va, gates, s0, m0):
    fwd = lambda w: pl.BlockSpec((1, SEG, w), lambda g, c: (g, c, 0))
    bwd = lambda w: pl.BlockSpec((1, SEG, w), lambda g, c: (g, SEGS_PER_GROUP - 1 - c, 0))
    st_shape = (1, 1, A_HEADS, A_QK_DIM, C_AUG)
    st_f = pl.BlockSpec(st_shape, lambda g, c: (g, c, 0, 0, 0))
    st_b = pl.BlockSpec(st_shape, lambda g, c: (g, SEGS_PER_GROUP - 1 - c, 0, 0, 0))
    m_f = pl.BlockSpec((1, 1, 1, LANES), lambda g, c: (g, c, 0, 0))
    m_b = pl.BlockSpec((1, 1, 1, LANES), lambda g, c: (g, SEGS_PER_GROUP - 1 - c, 0, 0))
    widths = (A_HEADS * A_QK_DIM, A_HEADS * A_QK_DIM, A_HEADS * A_V_DIM, LANES)
    st_out = jax.ShapeDtypeStruct((N_GROUPS, SEGS_PER_GROUP, A_HEADS, A_QK_DIM, C_AUG), F32)
    m_out = jax.ShapeDtypeStruct((N_GROUPS, SEGS_PER_GROUP, 1, LANES), F32)
    return pl.pallas_call(
        _mlstm_kernel,
        grid=(N_GROUPS, SEGS_PER_GROUP),
        in_specs=[fwd(w) for w in widths] + [bwd(w) for w in widths] + [
            pl.BlockSpec((1, 2, A_HEADS, A_QK_DIM, C_AUG), lambda g, c: (g, 0, 0, 0, 0)),
            pl.BlockSpec((1, 2, 1, LANES), lambda g, c: (g, 0, 0, 0))],
        out_specs=[pl.BlockSpec((1, GROUP_TOKENS, A_HEADS * A_V_DIM), lambda g, c: (g, 0, 0)),
                   st_f, st_b, m_f, m_b],
        out_shape=[jax.ShapeDtypeStruct((N_GROUPS, GROUP_TOKENS, A_HEADS * A_V_DIM), F32),
                   st_out, st_out, m_out, m_out],
        scratch_shapes=[pltpu.VMEM((2, A_HEADS, A_QK_DIM, C_AUG), F32), pltpu.VMEM((2, 1, LANES), F32)],
        compiler_params=_cparams(2),
        name="mlstm",
    )(qa, ka, va, gates, qa, ka, va, gates, s0, m0)


SUBLANES = 8


def _neg_expm1(y):
    series = 1.0 / 40320.0
    for k in (5040.0, 720.0, 120.0, 24.0, 6.0, 2.0, 1.0):
        series = series * y + 1.0 / k
    return jnp.where(y > -0.35, -(y * series), 1.0 - jnp.exp(y))


def _linear_scan(a, b, h_in, reverse):
    n = a.shape[0] // SUBLANES
    a3 = a.reshape(n, SUBLANES, D_RNN)
    b3 = b.reshape(n, SUBLANES, D_RNN)
    row = lax.broadcasted_iota(jnp.int32, a3.shape, 1)
    for d in (1, 2, 4):
        shift = SUBLANES - d if reverse else d
        valid = (row < SUBLANES - d) if reverse else (row >= d)
        b3 = jnp.where(valid, a3 * pltpu.roll(b3, shift, 1) + b3, b3)
        a3 = jnp.where(valid, a3 * pltpu.roll(a3, shift, 1), a3)
    h = h_in
    outs = [None] * n
    for j in (reversed(range(n)) if reverse else range(n)):
        hj = a3[j] * h + b3[j]
        outs[j] = hj
        h = hj[0:1] if reverse else hj[SUBLANES - 1:SUBLANES]
    return jnp.concatenate(outs, axis=0), h


def _rglru_kernel(xf_ref, pf_ref, nf_ref, xb_ref, pb_ref, nb_ref, cw_ref, cb_ref, wa_ref, wx_ref, ba_ref,
                  bx_ref, lam_ref, h0_ref, o_ref, sf_ref, sb_ref, h_scr):
    c, cb, starts, ends = _scan_flags()
    cw = cw_ref[...]

    def run(d, x_ref, p_ref, n_ref, chunk, start, end, first_touch):
        @pl.when(start if d == 0 else end)
        def _():
            h_scr[d] = h0_ref[0, d]

        prev = jnp.where(start, 0.0, p_ref[0])
        nxt = jnp.where(end, 0.0, n_ref[0])
        xcat = jnp.concatenate([prev, x_ref[0], nxt], axis=0)
        xc = cb_ref[...]
        for j in range(CONV_W):
            off = SUBLANES - CONV_LEFT + j
            xc = xc + cw[j:j + 1, :] * xcat[off:off + SEG, :]
        xcb = xc.astype(BF16)
        r = _sigmoid(_dot(xcb, wa_ref[d]) + ba_ref[d])
        i = _sigmoid(_dot(xcb, wx_ref[d]) + bx_ref[d])
        lam = lam_ref[d]
        softplus_neg = jnp.maximum(-lam, 0.0) + jnp.log1p(jnp.exp(-jnp.abs(lam)))
        log_a = (-RG_C * r) * softplus_neg
        a = jnp.exp(log_a)
        b = jnp.sqrt(_neg_expm1(2.0 * log_a)) * (i * xc)
        hs, h_last = _linear_scan(a, b, h_scr[d], reverse=(d == 1))
        h_scr[d] = h_last
        _store_or_add(o_ref, chunk, hs, first_touch)

    run(0, xf_ref, pf_ref, nf_ref, c, starts[0], ends[0], c < cb)
    run(1, xb_ref, pb_ref, nb_ref, cb, ends[1], starts[1], c < cb)
    sf_ref[0, 0] = h_scr[0]
    sb_ref[0, 0] = h_scr[1]


def _rglru(xr, conv_w, conv_b, wa, wx, ba, bx, lam, h0):
    blocks = SEG // SUBLANES
    n_blocks = GROUP_TOKENS // SUBLANES
    cbk = lambda c: SEGS_PER_GROUP - 1 - c
    x_f = pl.BlockSpec((1, SEG, D_RNN), lambda g, c: (g, c, 0))
    x_b = pl.BlockSpec((1, SEG, D_RNN), lambda g, c: (g, cbk(c), 0))
    halo = lambda f: pl.BlockSpec((1, SUBLANES, D_RNN), f)
    p_f = halo(lambda g, c: (g, jnp.maximum(c * blocks - 1, 0), 0))
    n_f = halo(lambda g, c: (g, jnp.minimum((c + 1) * blocks, n_blocks - 1), 0))
    p_b = halo(lambda g, c: (g, jnp.maximum(cbk(c) * blocks - 1, 0), 0))
    n_b = halo(lambda g, c: (g, jnp.minimum((cbk(c) + 1) * blocks, n_blocks - 1), 0))
    st = jax.ShapeDtypeStruct((N_GROUPS, SEGS_PER_GROUP, 1, D_RNN), F32)
    return pl.pallas_call(
        _rglru_kernel,
        grid=(N_GROUPS, SEGS_PER_GROUP),
        in_specs=[x_f, p_f, n_f, x_b, p_b, n_b, _const_spec(conv_w.shape), _const_spec(conv_b.shape),
                  _const_spec(wa.shape), _const_spec(wx.shape), _const_spec(ba.shape), _const_spec(bx.shape),
                  _const_spec(lam.shape), pl.BlockSpec((1, 2, 1, D_RNN), lambda g, c: (g, 0, 0, 0))],
        out_specs=[pl.BlockSpec((1, GROUP_TOKENS, D_RNN), lambda g, c: (g, 0, 0)),
                   pl.BlockSpec((1, 1, 1, D_RNN), lambda g, c: (g, c, 0, 0)),
                   pl.BlockSpec((1, 1, 1, D_RNN), lambda g, c: (g, cbk(c), 0, 0))],
        out_shape=[jax.ShapeDtypeStruct((N_GROUPS, GROUP_TOKENS, D_RNN), F32), st, st],
        scratch_shapes=[pltpu.VMEM((2, 1, D_RNN), F32)],
        compiler_params=_cparams(2),
        name="rglru",
    )(xr, xr, xr, xr, xr, xr, conv_w, conv_b, wa, wx, ba, bx, lam, h0)


OD_COLS = dict(cq=(0, 384), ckv=(384, 640), xr=(640, 1152), xg=(1152, 1664))
MLA_SCALE = (C_NOPE + C_ROPE) ** -0.5 * LOG2E


def _mla_keys(ckv_b, kr_b, wuk_ref, place_ref, wuv_ref):
    kcat = _dot(ckv_b, wuk_ref[...]) + _dot(kr_b, place_ref[...])
    v = _dot(ckv_b, wuv_ref[...])
    lane = lax.broadcasted_iota(jnp.int32, (1, v.shape[-1]), 1)
    return kcat.astype(BF16), (v + (lane % V_SLOT == C_V).astype(F32)).astype(BF16)


def _in_odd_kernel(x_ref, moe_ref, g2_ref, nw_ref, sc_ref, sh_ref, w_ref, wkr_ref, qnw_ref, kvnw_ref, wuq_ref,
                   wuk_ref, place_ref, wuv_ref, cq_ref, sq_ref, ck_ref, sk_ref,
                   x2_ref, q_ref, kcat_ref, v_ref, ckv_ref, kr_ref, xr_ref, xg_ref):
    x = x_ref[0] + g2_ref[0] * moe_ref[0]
    x2_ref[0] = x
    xm = _rms(x, nw_ref[...]) * (1.0 + sc_ref[0]) + sh_ref[0]
    xb = xm.astype(BF16)

    def proj(name):
        lo, hi = OD_COLS[name]
        return _dot(xb, w_ref[:, lo:hi])

    cq = _rms(proj("cq"), qnw_ref[...])
    q = _dot(cq.astype(BF16), wuq_ref[...])
    q = _rope(q, _tile_lanes(cq_ref[0], C_HEADS), _tile_lanes(sq_ref[0], C_HEADS), C_ROPE // 4)
    q_ref[0] = (q * MLA_SCALE).astype(BF16)
    ckv = _rms(proj("ckv"), kvnw_ref[...])
    ckv_ref[0] = ckv
    kr = _rope(_dot(xb, wkr_ref[...]), ck_ref[0], sk_ref[0], C_ROPE // 4)
    kr_ref[0] = kr
    kcat_ref[0], v_ref[0] = _mla_keys(ckv.astype(BF16), kr.astype(BF16), wuk_ref, place_ref, wuv_ref)
    xr_ref[0] = proj("xr")
    xg_ref[0] = proj("xg").astype(BF16)


def _in_odd(x, moe, g2, nw, sc, sh, w_main, w_kr, qnw, kvnw, w_uq, w_uk, place, w_uv, cos_q, sin_q, cos_k, sin_k):
    tm = TOKEN_TILE
    widths = [(D_MODEL, F32), (C_HEADS * C_QPAD, BF16), (C_HEADS * C_QPAD, BF16), (C_HEADS * V_SLOT, BF16),
              (C_KV_LORA, F32), (LANES, F32), (D_RNN, F32), (D_RNN, BF16)]
    consts = [nw, None, None, w_main, w_kr, qnw, kvnw, w_uq, w_uk, place, w_uv]
    const_specs = [_mod_spec() if a is None else _const_spec(a.shape) for a in consts]
    return pl.pallas_call(
        _in_odd_kernel,
        grid=(N_GROUPS, GROUP_TOKENS // tm),
        in_specs=[_grp_spec(tm, D_MODEL), _grp_spec(tm, D_MODEL), _mod_spec()] + const_specs +
                 [_rope_spec(tm)] * 4,
        out_specs=[_grp_spec(tm, w) for w, _ in widths],
        out_shape=[jax.ShapeDtypeStruct((N_GROUPS, GROUP_TOKENS, w), dt) for w, dt in widths],
        compiler_params=_cparams(2),
        name="in_odd",
    )(x, moe, g2, nw, sc, sh, w_main, w_kr, qnw, kvnw, w_uq, w_uk, place, w_uv, cos_q, sin_q, cos_k, sin_k)


def _kv_expand_kernel(ckv_ref, kr_ref, wuk_ref, place_ref, wuv_ref, kcat_ref, v_ref):
    kcat_ref[0], v_ref[0] = _mla_keys(ckv_ref[0].astype(BF16), kr_ref[0].astype(BF16), wuk_ref, place_ref, wuv_ref)


def _kv_expand(ckv, kr_pad, w_uk, place, w_uv):
    n = ckv.shape[0]
    cst = lambda a: pl.BlockSpec(a.shape, lambda b: (0,) * a.ndim)
    return pl.pallas_call(
        _kv_expand_kernel,
        grid=(n,),
        in_specs=[pl.BlockSpec((1, PAST_LEN, C_KV_LORA), lambda b: (b, 0, 0)),
                  pl.BlockSpec((1, PAST_LEN, LANES), lambda b: (b, 0, 0)), cst(w_uk), cst(place), cst(w_uv)],
        out_specs=[pl.BlockSpec((1, PAST_LEN, C_HEADS * C_QPAD), lambda b: (b, 0, 0)),
                   pl.BlockSpec((1, PAST_LEN, C_HEADS * V_SLOT), lambda b: (b, 0, 0))],
        out_shape=[jax.ShapeDtypeStruct((n, PAST_LEN, C_HEADS * C_QPAD), BF16),
                   jax.ShapeDtypeStruct((n, PAST_LEN, C_HEADS * V_SLOT), BF16)],
        compiler_params=_cparams(1),
        name="kv_expand",
    )(ckv, kr_pad, w_uk, place, w_uv)


def _route_kernel(aff_ref, idx_ref, cum_scr):
    is_ctx = pl.program_id(0) >= N_SAMPLE_GROUPS
    cap = jnp.where(is_ctx, CTX_CAP, SLOTS).astype(F32)
    aff = aff_ref[0].reshape(SEGS_PER_GROUP, SEG, LANES)

    def per_set(per_seg, combine):
        whole = jnp.broadcast_to(combine(per_seg, axis=0, keepdims=True), per_seg.shape)
        return jnp.where(is_ctx, per_seg, whole)

    def count(mask):
        return per_set(jnp.sum(mask.astype(F32), axis=1), jnp.sum)

    def as_float(word):
        return lax.bitcast_convert_type(word, F32)[:, None, :]

    def search(i, word):
        cand = word | lax.shift_left(jnp.int32(1), 30 - i)
        return jnp.where(count(aff >= as_float(cand)) >= cap, cand, word)

    word = lax.fori_loop(0, 31, search, jnp.zeros((SEGS_PER_GROUP, LANES), jnp.int32))
    upper = as_float(word + 1)
    kth = as_float(word)
    left = cap - count(aff >= upper)
    found = jnp.zeros(left.shape, jnp.bool_)
    for _ in range(3):
        cand = per_set(jnp.max(jnp.where(aff < upper, aff, -1.0), axis=1), jnp.max)[:, None, :]
        n_cand = count(aff == cand)
        hit = jnp.logical_and(jnp.logical_not(found), left <= n_cand)
        kth = jnp.where(hit[:, None, :], cand, kth)
        found = jnp.logical_or(found, hit)
        left = jnp.where(found, left, left - n_cand)
        upper = jnp.where(found[:, None, :], upper, cand)
    above = aff > kth
    tied = aff == kth
    need = cap - count(above)

    tri = (lax.broadcasted_iota(jnp.int32, (SEG, SEG), 1) <= lax.broadcasted_iota(jnp.int32, (SEG, SEG), 0)).astype(BF16)

    def prefix(mask, across_segments):
        m = mask.astype(BF16)
        outs = []
        offset = jnp.zeros((1, LANES), F32)
        for s in range(SEGS_PER_GROUP):
            p = _dot(tri, m[s])
            outs.append(p + jnp.where(across_segments, offset, 0.0))
            offset = offset + p[SEG - 1:SEG, :]
        return jnp.stack(outs)

    tied_rank = prefix(tied, jnp.logical_not(is_ctx)) - tied.astype(F32)
    keep = above | (tied & (tied_rank < need[:, None, :]))
    cum_scr[...] = prefix(keep, True)

    slot = lax.broadcasted_iota(jnp.int32, (1, SLOTS), 1).astype(F32)
    for e in range(N_EXPERTS):
        def block(s, acc):
            ce = cum_scr[s][:, e:e + 1]
            return acc + jnp.sum((ce <= slot).astype(F32), axis=0, keepdims=True)
        n_before = lax.fori_loop(0, SEGS_PER_GROUP, block, jnp.zeros((1, SLOTS), F32))
        idx_ref[0, e:e + 1, :] = n_before.astype(jnp.int32)


def _route(aff):
    return pl.pallas_call(
        _route_kernel,
        grid=(N_GROUPS,),
        in_specs=[pl.BlockSpec((1, GROUP_TOKENS, LANES), lambda g: (g, 0, 0))],
        out_specs=pl.BlockSpec((1, N_EXPERTS, SLOTS), lambda g: (g, 0, 0)),
        out_shape=jax.ShapeDtypeStruct((N_GROUPS, N_EXPERTS, SLOTS), jnp.int32),
        scratch_shapes=[pltpu.VMEM((SEGS_PER_GROUP, SEG, LANES), F32)],
        compiler_params=_cparams(1),
        name="route",
    )(aff)


def _moe_kernel(idx_ref, xpk_ref, aff_ref, wg_ref, wu_ref, wd_ref, o_ref, xe_scr, ga_scr, ye_scr):
    g = pl.program_id(0)
    e = pl.program_id(1)
    base = (g * N_EXPERTS + e) * SLOTS

    @pl.when(e == 0)
    def _():
        o_ref[...] = jnp.zeros_like(o_ref)

    def gather(s, carry):
        t = idx_ref[base + s]
        xe_scr[pl.ds(s, 1), :] = xpk_ref[0, pl.ds(t, 1), :]
        ga_scr[pl.ds(s, 1), :] = aff_ref[0, pl.ds(t, 1), :]
        return carry

    lax.fori_loop(0, SLOTS, gather, 0, unroll=8)
    xe = xe_scr[...].astype(BF16)
    lane = lax.broadcasted_iota(jnp.int32, (SLOTS, LANES), 1)
    gate = jnp.sum(jnp.where(lane == e, ga_scr[...], 0.0), axis=-1, keepdims=True)
    a = _dot(xe, wg_ref[0])
    hid = (a * _sigmoid(a)) * _dot(xe, wu_ref[0])
    ye_scr[...] = _dot(hid.astype(BF16), wd_ref[0]) * gate

    def scatter(s, carry):
        t = idx_ref[base + s]
        o_ref[0, pl.ds(t, 1), :] += ye_scr[pl.ds(s, 1), :]
        return carry

    lax.fori_loop(0, SLOTS, scatter, 0, unroll=8)


def _moe(idx, xpk, aff, w_gate, w_up, w_down):
    once = pl.Buffered(1)
    return pl.pallas_call(
        _moe_kernel,
        grid_spec=pltpu.PrefetchScalarGridSpec(
            num_scalar_prefetch=1,
            grid=(N_GROUPS, N_EXPERTS),
            in_specs=[
                pl.BlockSpec((1, GROUP_TOKENS, D_MODEL), lambda g, e, idx: (g, 0, 0), pipeline_mode=once),
                pl.BlockSpec((1, GROUP_TOKENS, LANES), lambda g, e, idx: (g, 0, 0), pipeline_mode=once),
                pl.BlockSpec((1, D_MODEL, EXPERT_FF), lambda g, e, idx: (e, 0, 0)),
                pl.BlockSpec((1, D_MODEL, EXPERT_FF), lambda g, e, idx: (e, 0, 0)),
                pl.BlockSpec((1, EXPERT_FF, D_MODEL), lambda g, e, idx: (e, 0, 0)),
            ],
            out_specs=pl.BlockSpec((1, GROUP_TOKENS, D_MODEL), lambda g, e, idx: (g, 0, 0), pipeline_mode=once),
            scratch_shapes=[pltpu.VMEM((SLOTS, D_MODEL), F32), pltpu.VMEM((SLOTS, LANES), F32),
                            pltpu.VMEM((SLOTS, D_MODEL), F32)],
        ),
        out_shape=jax.ShapeDtypeStruct((N_GROUPS, GROUP_TOKENS, D_MODEL), F32),
        compiler_params=_cparams(2),
        name="moe",
    )(idx.reshape(-1), xpk, aff, w_gate, w_up, w_down)


def _final_kernel(x_ref, moe_ref, g2_ref, nw_ref, o_ref):
    o_ref[0] = _rms(x_ref[0] + g2_ref[0] * moe_ref[0], nw_ref[...])


def _final(x, moe, g2, nw, first_group, n_groups):
    tm = TOKEN_TILE
    src = lambda w: pl.BlockSpec((1, tm, w), lambda g, i: (g + first_group, i, 0))
    return pl.pallas_call(
        _final_kernel,
        grid=(n_groups, GROUP_TOKENS // tm),
        in_specs=[src(D_MODEL), src(D_MODEL), pl.BlockSpec((1, 1, D_MODEL), lambda g, i: (g + first_group, 0, 0)),
                  _const_spec(nw.shape)],
        out_specs=_grp_spec(tm, D_MODEL),
        out_shape=jax.ShapeDtypeStruct((n_groups, GROUP_TOKENS, D_MODEL), F32),
        compiler_params=_cparams(2),
        name="final_norm",
    )(x, moe, g2, nw)


def _hi_lo(w):
    hi = w.astype(BF16)
    return hi, (w - hi.astype(F32)).astype(BF16)


def _pad_cols(w, width=LANES):
    return jnp.pad(w, ((0, 0),) * (w.ndim - 1) + ((0, width - w.shape[-1]),))


def _ctx_only(a):
    tail = a[N_SAMPLE_GROUPS:]
    return tail.reshape((N_CTX_GROUPS * SEGS_PER_GROUP,) + tail.shape[2:])


def _with_ctx_zeros(a):
    return jnp.concatenate([a, jnp.zeros((N_CTX_GROUPS,) + a.shape[1:], a.dtype)], axis=0)


def _moe_block(xpk, aff, w_gate, w_up, w_down):
    idx = _route(aff)
    return _moe(idx, xpk, aff, w_gate.astype(BF16), w_up.astype(BF16), w_down.astype(BF16))


def kernel(x_prompt, x_sample, state_mlstm_C, state_mlstm_n, state_mlstm_m, cache_gqa_k, cache_gqa_v, cache_mla_ckv, cache_mla_krope, state_rglru_h, c, c_ctx, norm1_w, norm2_w, final_norm_w, ada_w, ada_b, ev_w_in, ev_w_out, gqa_q_norm, gqa_k_norm, mlstm_gate_b, mlstm_norm_w, od_w_in, od_w_out, mla_q_norm, mla_kv_norm, mla_w_uq, mla_w_ukv, rg_conv_w, rg_conv_b, rg_wa, rg_ba, rg_wx, rg_bx, rg_lambda, router_w, exp_w_gate, exp_w_up, exp_w_down):
    n_ctx = x_prompt.shape[0]
    xs = x_sample
    xp = x_prompt.reshape(N_CTX_GROUPS, GROUP_TOKENS, D_MODEL)

    cond = jnp.concatenate([c, jnp.broadcast_to(c_ctx[None, :], (16 - N_SAMPLE_GROUPS, D_MODEL))], axis=0)
    mods = _adaln(cond, ada_w, ada_b)[:, :N_GROUPS]
    mod = [[m[:, None, :] for m in jnp.split(mods[layer], 6, axis=-1)] for layer in range(2)]
    router = [_hi_lo(_pad_cols(router_w[layer])) for layer in range(2)]

    sh1, sc1, g1, sh2, sc2, g2 = mod[0]
    w_in = ev_w_in[0]
    wg_hi, wg_lo = _hi_lo(_pad_cols(w_in[:, EV_MAIN:]))
    bd = jnp.asarray(np.kron(np.eye(B_HEADS), np.full((B_HEAD_DIM, B_HEAD_DIM), 1.0 / B_HEAD_DIM)), BF16)
    cos_b, sin_b = _rope_tables(B_HEAD_DIM)
    cos_b, sin_b = _pad_tables(np.tile(cos_b, (1, 2)), np.tile(sin_b, (1, 2)))
    v_lo, v_hi = EV_COLS["v"]
    w_vaug = jnp.pad(w_in[:, v_lo:v_hi].reshape(D_MODEL, B_KV_HEADS, B_HEAD_DIM),
                     ((0, 0), (0, 0), (0, V_SLOT - B_HEAD_DIM))).reshape(D_MODEL, -1).astype(BF16)
    slot_one = np.zeros((1, B_KV_HEADS, V_SLOT), np.float32)
    slot_one[:, :, B_HEAD_DIM] = 1.0
    q, kb, vb, qa, ka, va, oa, ga, k0, k1, v_aug = _in_even(
        xs, xp, norm1_w[0][None, :], sc1, sh1, w_in[:, :EV_MAIN].astype(BF16), wg_hi, wg_lo,
        _pad_cols(mlstm_gate_b[0][None, :]), bd, jnp.tile(gqa_q_norm[0], B_HEADS)[None, :],
        jnp.tile(gqa_k_norm[0], B_KV_HEADS)[None, :], cos_b, sin_b, w_vaug,
        jnp.asarray(slot_one.reshape(1, -1)))
    kc = jnp.swapaxes(cache_gqa_k[:, 0], 1, 2).astype(BF16)
    vc = jnp.pad(cache_gqa_v[:, 0], ((0, 0), (0, 0), (0, 0), (0, V_SLOT - B_HEAD_DIM))) + jnp.asarray(slot_one)
    att = _gqa_attention(q, k0, k1, v_aug, kc, vc.reshape(N_SAMPLE_GROUPS, PAST_LEN, -1).astype(BF16))
    s0 = jnp.concatenate([state_mlstm_C[:, 0], state_mlstm_n[:, 0][..., None],
                          jnp.zeros(state_mlstm_C[:, 0].shape[:-1] + (C_AUG - A_V_DIM - 1,), F32)], axis=-1)
    m0 = _pad_cols(state_mlstm_m[:, 0])[:, :, None, :]
    h_a, st_f, st_b, m_f, m_b = _mlstm(qa, ka, va, ga, _with_ctx_zeros(s0), _with_ctx_zeros(m0))
    x1, xpk, aff = _out_proj((xs, xp), att, h_a, oa, g1, ev_w_out[0].astype(BF16), mlstm_norm_w[0][None, :],
                             norm2_w[0][None, :], sc2, sh2, *router[0], even=True)
    moe0 = _moe_block(xpk, aff, exp_w_gate[0], exp_w_up[0], exp_w_down[0])
    g2_0 = g2

    sh1, sc1, g1, sh2, sc2, g2 = mod[1]
    w_in = od_w_in[0]
    o_kr = C_Q_LORA + C_KV_LORA
    w_main = jnp.concatenate([w_in[:, :o_kr], w_in[:, o_kr + C_ROPE:]], axis=1).astype(BF16)
    w_kr = _pad_cols(w_in[:, o_kr:o_kr + C_ROPE]).astype(BF16)
    uq = mla_w_uq[0].reshape(C_Q_LORA, C_HEADS, C_NOPE + C_ROPE)
    w_uq = jnp.pad(uq, ((0, 0), (0, 0), (0, C_QPAD - C_NOPE - C_ROPE))).reshape(C_Q_LORA, -1).astype(BF16)
    ukv = mla_w_ukv[0].reshape(C_KV_LORA, C_HEADS, C_NOPE + C_V)
    w_uk = jnp.pad(ukv[..., :C_NOPE], ((0, 0), (0, 0), (0, C_QPAD - C_NOPE))).reshape(C_KV_LORA, -1).astype(BF16)
    w_uv = jnp.pad(ukv[..., C_NOPE:], ((0, 0), (0, 0), (0, V_SLOT - C_V))).reshape(C_KV_LORA, -1).astype(BF16)
    place_np = np.zeros((LANES, C_HEADS, C_QPAD), np.float32)
    for j in range(C_ROPE):
        place_np[j, :, C_NOPE + j] = 1.0
    place = jnp.asarray(place_np.reshape(LANES, -1), BF16)
    cos_c, sin_c = _rope_tables(C_ROPE)
    cos_k, sin_k = _pad_tables(cos_c, sin_c)
    cos_q = np.ones((GROUP_TOKENS, C_NOPE + C_ROPE), np.float32)
    sin_q = np.zeros((GROUP_TOKENS, C_NOPE + C_ROPE), np.float32)
    cos_q[:, C_NOPE:] = cos_c
    sin_q[:, C_NOPE:] = sin_c
    cos_q, sin_q = _pad_tables(cos_q, sin_q)
    x2, q, kcat, v, ckv, kr, xr, xg = _in_odd(
        x1, moe0, g2_0, norm1_w[1][None, :], sc1, sh1, w_main, w_kr, mla_q_norm[0][None, :],
        mla_kv_norm[0][None, :], w_uq, w_uk, place, w_uv, cos_q, sin_q, cos_k, sin_k)
    kcat_c, v_c = _kv_expand(cache_mla_ckv[:, 0], _pad_cols(cache_mla_krope[:, 0].reshape(-1, C_ROPE)).reshape(
        N_SAMPLE_GROUPS, PAST_LEN, LANES), w_uk, place, w_uv)
    att = _mla_attention(q, kcat, v, kcat_c, v_c)
    eye = jnp.eye(RG_BLOCKS, dtype=F32)
    dense = lambda w: jnp.einsum("knde,nm->kndme", w, eye).reshape(2, D_RNN, D_RNN).astype(BF16)
    h0 = _with_ctx_zeros(state_rglru_h[:, 0][:, :, None, :])
    rg, rs_f, rs_b = _rglru(xr, rg_conv_w[0], rg_conv_b[0][None, :], dense(rg_wa[0]), dense(rg_wx[0]),
                            rg_ba[0][:, None, :], rg_bx[0][:, None, :], rg_lambda[0][:, None, :], h0)
    x3, xpk, aff = _out_proj((x2,), att, rg, xg, g1, od_w_out[0].astype(BF16), None, norm2_w[1][None, :],
                             sc2, sh2, *router[1], even=False)
    moe1 = _moe_block(xpk, aff, exp_w_gate[1], exp_w_up[1], exp_w_down[1])

    fnw = final_norm_w[None, :]
    y_sample = _final(x3, moe1, g2, fnw, 0, N_SAMPLE_GROUPS)
    y_prompt = _final(x3, moe1, g2, fnw, N_SAMPLE_GROUPS, N_CTX_GROUPS).reshape(n_ctx, SEG, D_MODEL)

    st = jnp.stack([_ctx_only(st_f), _ctx_only(st_b)], axis=1)
    new_c = st[..., :A_V_DIM][:, None]
    new_n = st[..., A_V_DIM][:, None]
    new_m = jnp.stack([_ctx_only(m_f)[:, 0, :A_HEADS], _ctx_only(m_b)[:, 0, :A_HEADS]], axis=1)[:, None]
    new_gk = kb[N_SAMPLE_GROUPS:].reshape(n_ctx, 1, SEG, B_KV_HEADS, B_HEAD_DIM)
    new_gv = vb[N_SAMPLE_GROUPS:].reshape(n_ctx, 1, SEG, B_KV_HEADS, B_HEAD_DIM)
    new_ckv = ckv[N_SAMPLE_GROUPS:].reshape(n_ctx, 1, SEG, C_KV_LORA)
    new_kr = kr[N_SAMPLE_GROUPS:, :, :C_ROPE].reshape(n_ctx, 1, SEG, C_ROPE)
    new_rh = jnp.stack([_ctx_only(rs_f)[:, 0], _ctx_only(rs_b)[:, 0]], axis=1)[:, None]
    return (y_prompt, y_sample, new_c, new_n, new_m, new_gk, new_gv, new_ckv, new_kr, new_rh)
```

```python
import functools

import numpy as np
import jax
import jax.numpy as jnp
from jax import lax
from jax.experimental import pallas as pl
from jax.experimental.pallas import tpu as pltpu

F32 = jnp.float32
BF16 = jnp.bfloat16

D_MODEL = 1024
N_SAMPLE_GROUPS = 8
N_CTX_GROUPS = 2
N_GROUPS = N_SAMPLE_GROUPS + N_CTX_GROUPS
GROUP_TOKENS = 4096
SEG = 256
SEGS_PER_GROUP = GROUP_TOKENS // SEG
PAST_LEN = 512
GRID_W = 64
EPS = 1e-6
ROPE_BASE = 10000.0

B_HEADS, B_KV_HEADS, B_HEAD_DIM = 8, 2, 64
A_HEADS, A_QK_DIM, A_V_DIM = 4, 64, 128
C_HEADS, C_Q_LORA, C_KV_LORA, C_NOPE, C_ROPE, C_V = 8, 384, 256, 64, 32, 64
C_QPAD = 128
D_RNN, RG_BLOCKS, RG_C, CONV_W, CONV_LEFT = 512, 8, 8.0, 4, 2
N_EXPERTS, EXPERT_FF = 16, 512
SLOTS = 512
CTX_CAP = 2 * SEG // N_EXPERTS
LANES = 128
SUBLANES = 8
TOKEN_TILE = 512
VMEM_LIMIT = 56 * 1024 * 1024


def _cparams(n_axes, vmem=VMEM_LIMIT):
    return pltpu.CompilerParams(dimension_semantics=("arbitrary",) * n_axes, vmem_limit_bytes=vmem)


def _dot(a, b):
    return jnp.dot(a, b, preferred_element_type=F32)


def _dot_nt(a, b):
    return lax.dot_general(a, b, (((1,), (1,)), ((), ())), preferred_element_type=F32)


def _split3(a):
    hi = a.astype(BF16)
    r = a - hi.astype(F32)
    mid = r.astype(BF16)
    lo = (r - mid.astype(F32)).astype(BF16)
    return hi, mid, lo


def _dot_x3(a, w_hi, w_lo):
    a_hi = a.astype(BF16)
    a_lo = (a - a_hi.astype(F32)).astype(BF16)
    return _dot(a_hi, w_hi) + (_dot(a_lo, w_hi) + _dot(a_hi, w_lo))


def _rms(x, w):
    return (x * lax.rsqrt(jnp.mean(x * x, axis=-1, keepdims=True) + EPS)) * w


def _sigmoid(x):
    return 1.0 / (1.0 + jnp.exp(-x))


def _rope(x, cos, sin_signed, half):
    n = x.shape[-1]
    lane = lax.broadcasted_iota(jnp.int32, x.shape, x.ndim - 1)
    first = (lane % (2 * half)) < half
    partner = jnp.where(first, pltpu.roll(x, n - half, x.ndim - 1), pltpu.roll(x, half, x.ndim - 1))
    return x * cos + partner * sin_signed


def _tile_lanes(a, reps):
    return a if reps == 1 else jnp.concatenate([a] * reps, axis=-1)


def _adaln_kernel(c_ref, w_ref, b_ref, o_ref):
    c = c_ref[...]
    a = c * _sigmoid(c)
    w = w_ref[0]
    w_hi = w.astype(BF16)
    w_lo = (w - w_hi.astype(F32)).astype(BF16)
    o_ref[0] = _dot_x3(a, w_hi, w_lo) + b_ref[0]


def _adaln(cond, ada_w, ada_b):
    depth, _, n6 = ada_w.shape
    rows = cond.shape[0]
    tn = 1024
    return pl.pallas_call(
        _adaln_kernel,
        grid=(depth, n6 // tn),
        in_specs=[
            pl.BlockSpec((rows, D_MODEL), lambda l, j: (0, 0)),
            pl.BlockSpec((1, D_MODEL, tn), lambda l, j: (l, 0, j)),
            pl.BlockSpec((1, 1, tn), lambda l, j: (l, 0, j)),
        ],
        out_specs=pl.BlockSpec((1, rows, tn), lambda l, j: (l, 0, j)),
        out_shape=jax.ShapeDtypeStruct((depth, rows, n6), F32),
        compiler_params=_cparams(2),
        name="adaln",
    )(cond, ada_w, ada_b.reshape(depth, 1, n6))


def _grp_spec(tm, width):
    return pl.BlockSpec((1, tm, width), lambda g, i: (g, i, 0))


def _mod_spec():
    return pl.BlockSpec((1, 1, D_MODEL), lambda g, i: (g, 0, 0))


def _const_spec(shape):
    nd = len(shape)
    return pl.BlockSpec(shape, lambda g, i: (0,) * nd)


def _two_source_specs(tm):
    last = GROUP_TOKENS // tm - 1
    xs = pl.BlockSpec((1, tm, D_MODEL), lambda g, i: (jnp.minimum(g, N_SAMPLE_GROUPS - 1),
                                                      jnp.where(g < N_SAMPLE_GROUPS, i, last), 0))
    xp = pl.BlockSpec((1, tm, D_MODEL), lambda g, i: (jnp.maximum(g - N_SAMPLE_GROUPS, 0),
                                                      jnp.where(g < N_SAMPLE_GROUPS, 0, i), 0))
    return xs, xp


def _rope_spec(tm):
    return pl.BlockSpec((1, tm, LANES), lambda g, i: (jnp.where(g < N_SAMPLE_GROUPS, 0, 1), i, 0))


def _rope_tables(d):
    half = d // 4
    t = np.arange(GROUP_TOKENS)
    pos = np.stack([(t // GRID_W).astype(np.float32), (t % GRID_W).astype(np.float32)], axis=1)
    lane = np.arange(d)
    axis = lane // (d // 2)
    j = lane % half
    first = (lane % (d // 2)) < half
    inv_freq = (np.float32(ROPE_BASE) ** (-(np.arange(half, dtype=np.float32)) / np.float32(half))).astype(np.float32)
    ang = (pos[:, axis] * inv_freq[j][None, :]).astype(np.float32)
    cos = np.cos(ang.astype(np.float64)).astype(np.float32)
    sin = np.sin(ang.astype(np.float64)).astype(np.float32)
    sin = np.where(first[None, :], -sin, sin)
    return cos, sin


def _pad_tables(cos, sin):
    t, w = cos.shape
    cos_p = np.ones((t, LANES), np.float32)
    sin_p = np.zeros((t, LANES), np.float32)
    cos_p[:, :w] = cos
    sin_p[:, :w] = sin
    cos2 = np.stack([cos_p, np.ones_like(cos_p)])
    sin2 = np.stack([sin_p, np.zeros_like(sin_p)])
    return jnp.asarray(cos2), jnp.asarray(sin2)


EV_COLS = dict(q=(0, 512), k=(512, 640), v=(640, 768), qa=(768, 1024), ka=(1024, 1280), va=(1280, 1792),
               oa=(1792, 2304))
EV_MAIN = 2304


LOG2E = 1.4426950408889634
V_SLOT = LANES


def _in_even_kernel(xs_ref, xp_ref, nw_ref, sc_ref, sh_ref, w_ref, wgh_ref, wgl_ref, gb_ref, bd_ref, qnw_ref,
                    knw_ref, cos_ref, sin_ref, wv_ref, vone_ref,
                    q_ref, kb_ref, vb_ref, qa_ref, ka_ref, va_ref, oa_ref, ga_ref, k0_ref, k1_ref, vaug_ref):
    g = pl.program_id(0)
    x = jnp.where(g >= N_SAMPLE_GROUPS, xp_ref[0], xs_ref[0])
    xm = _rms(x, nw_ref[...]) * (1.0 + sc_ref[0]) + sh_ref[0]
    xb = xm.astype(BF16)
    cos = cos_ref[0]
    sin = sin_ref[0]

    def proj(name):
        lo, hi = EV_COLS[name]
        return _dot(xb, w_ref[:, lo:hi])

    q = proj("q")
    q = q * lax.rsqrt(_dot((q * q).astype(BF16), bd_ref[...]) + EPS) * qnw_ref[...]
    q = _rope(q, _tile_lanes(cos, 4), _tile_lanes(sin, 4), B_HEAD_DIM // 4)
    q_ref[0] = (q * (B_HEAD_DIM ** -0.5 * LOG2E)).astype(BF16)
    k = proj("k")
    k = k * lax.rsqrt(_dot((k * k).astype(BF16), bd_ref[0:LANES, 0:LANES]) + EPS) * knw_ref[...]
    k = _rope(k, cos, sin, B_HEAD_DIM // 4)
    kb_ref[0] = k
    k0_ref[0] = k[:, :B_HEAD_DIM].astype(BF16)
    k1_ref[0] = k[:, B_HEAD_DIM:].astype(BF16)
    vb_ref[0] = proj("v")
    vaug_ref[0] = (_dot(xb, wv_ref[...]) + vone_ref[...]).astype(BF16)
    qa_ref[0] = proj("qa").astype(BF16)
    ka_ref[0] = (proj("ka") * (A_QK_DIM ** -0.5)).astype(BF16)
    va_ref[0] = proj("va").astype(BF16)
    oa_ref[0] = proj("oa").astype(BF16)
    ga_ref[0] = _dot_x3(xm, wgh_ref[...], wgl_ref[...]) + gb_ref[...]


def _in_even(xs, xp, nw, sc, sh, w_main, wg_hi, wg_lo, gb, bd, qnw, knw, cos, sin, w_vaug, v_one):
    tm = TOKEN_TILE
    xs_spec, xp_spec = _two_source_specs(tm)
    widths = [(512, BF16), (128, F32), (128, F32), (256, BF16), (256, BF16), (512, BF16), (512, BF16), (128, F32),
              (B_HEAD_DIM, BF16), (B_HEAD_DIM, BF16), (B_KV_HEADS * V_SLOT, BF16)]
    return pl.pallas_call(
        _in_even_kernel,
        grid=(N_GROUPS, GROUP_TOKENS // tm),
        in_specs=[xs_spec, xp_spec, _const_spec((1, D_MODEL)), _mod_spec(), _mod_spec(),
                  _const_spec(w_main.shape), _const_spec(wg_hi.shape), _const_spec(wg_lo.shape),
                  _const_spec(gb.shape), _const_spec(bd.shape), _const_spec(qnw.shape), _const_spec(knw.shape),
                  _rope_spec(tm), _rope_spec(tm), _const_spec(w_vaug.shape), _const_spec(v_one.shape)],
        out_specs=[_grp_spec(tm, w) for w, _ in widths],
        out_shape=[jax.ShapeDtypeStruct((N_GROUPS, GROUP_TOKENS, w), dt) for w, dt in widths],
        compiler_params=_cparams(2),
        name="in_even",
    )(xs, xp, nw, sc, sh, w_main, wg_hi, wg_lo, gb, bd, qnw, knw, cos, sin, w_vaug, v_one)


ATT_CHUNK = 512


def _kv_source(k_ref, k_idx, v_ref, v_idx, n, first=0):
    return n, (lambda rows: k_ref[k_idx(rows)]), (lambda rows: v_ref[v_idx(rows)]), first


def _softmax_pv_heads(heads, dv):
    def chunks_of(sources):
        out = []
        for n, load_k, load_v, first in sources:
            step = min(n, ATT_CHUNK)
            out += [(load_k, load_v, pl.ds(first + c, step)) for c in range(0, n, step)]
        return out

    plans = [(q, chunks_of(sources)) for q, sources in heads]
    scores = [[] for _ in plans]
    tile_max = [None] * len(plans)
    accs = [None] * len(plans)

    def score_chunk(h, i):
        q, chunks = plans[h]
        if i >= len(chunks):
            return
        s = _dot_nt(q, chunks[i][0](chunks[i][2]))
        scores[h].append(s)
        for j in range(0, s.shape[1], LANES):
            t = s[:, j:j + LANES]
            tile_max[h] = t if tile_max[h] is None else jnp.maximum(tile_max[h], t)

    def value_chunk(h, i, m):
        _, chunks = plans[h]
        if i >= len(chunks):
            return
        pv = _dot(jnp.exp2(scores[h][i] - m).astype(BF16), chunks[i][1](chunks[i][2]))
        accs[h] = pv if accs[h] is None else accs[h] + pv

    n_chunks = max(len(c) for _, c in plans)
    for i in range(n_chunks):
        score_chunk(0, i)
    for h in range(len(plans)):
        m = jnp.max(tile_max[h], axis=-1, keepdims=True)
        for i in range(n_chunks):
            value_chunk(h, i, m)
            if h + 1 < len(plans):
                score_chunk(h + 1, i)
    return [acc[:, :dv] / acc[:, dv:dv + 1] for acc in accs]


def _attn_branches(g, emit):
    @pl.when(g < N_SAMPLE_GROUPS)
    def _():
        emit(True)

    @pl.when(g >= N_SAMPLE_GROUPS)
    def _():
        emit(False)


def _gqa_kernel(q_ref, k0_ref, k1_ref, v_ref, kc_ref, vc_ref, o_ref):
    own = pl.multiple_of(pl.program_id(1) * SEG, SEG)
    q = q_ref[0]
    pair = 2 * B_HEAD_DIM

    def sources_of(kv, k_ref, latent):
        vl = slice(kv * V_SLOT, (kv + 1) * V_SLOT)
        new = functools.partial(_kv_source, k_ref, lambda r: (0, r), v_ref, lambda r: (0, r, vl))
        if not latent:
            return [new(SEG, own)]
        return [_kv_source(kc_ref, lambda r: (0, kv, r), vc_ref, lambda r: (0, r, vl), PAST_LEN),
                new(GROUP_TOKENS)]

    def emit(latent):
        heads = []
        for kv, k_ref in enumerate((k0_ref, k1_ref)):
            sources = sources_of(kv, k_ref, latent)
            for half in range(B_HEADS // B_KV_HEADS // 2):
                lo = (kv * B_HEADS // B_KV_HEADS + 2 * half) * B_HEAD_DIM
                heads.append((jnp.concatenate([q[:, lo:lo + B_HEAD_DIM], q[:, lo + B_HEAD_DIM:lo + pair]], axis=0),
                              sources))
        outs = []
        for o in _softmax_pv_heads(heads, B_HEAD_DIM):
            outs += [o[:SEG], o[SEG:]]
        o_ref[0] = jnp.concatenate(outs, axis=-1).astype(BF16)

    _attn_branches(pl.program_id(0), emit)


def _gqa_attention(q, k0, k1, v_aug, kc, vc):
    cache = lambda nd: (lambda g, c: (jnp.minimum(g, N_SAMPLE_GROUPS - 1),) + (0,) * (nd - 1))
    whole = lambda w: pl.BlockSpec((1, GROUP_TOKENS, w), lambda g, c: (g, 0, 0))
    return pl.pallas_call(
        _gqa_kernel,
        grid=(N_GROUPS, SEGS_PER_GROUP),
        in_specs=[pl.BlockSpec((1, SEG, B_HEADS * B_HEAD_DIM), lambda g, c: (g, c, 0)),
                  whole(B_HEAD_DIM), whole(B_HEAD_DIM), whole(B_KV_HEADS * V_SLOT),
                  pl.BlockSpec((1, B_KV_HEADS, PAST_LEN, B_HEAD_DIM), cache(4)),
                  pl.BlockSpec((1, PAST_LEN, B_KV_HEADS * V_SLOT), cache(3))],
        out_specs=pl.BlockSpec((1, SEG, B_HEADS * B_HEAD_DIM), lambda g, c: (g, c, 0)),
        out_shape=jax.ShapeDtypeStruct((N_GROUPS, GROUP_TOKENS, B_HEADS * B_HEAD_DIM), BF16),
        compiler_params=_cparams(2),
        name="gqa_attention",
    )(q, k0, k1, v_aug, kc, vc)


MLA_PAIR = 4


def _mla_kernel(q_ref, k_ref, v_ref, kc_ref, vc_ref, o_ref):
    own = pl.multiple_of(pl.program_id(2) * SEG, SEG)
    q = q_ref[0]

    def sources_of(j, latent):
        kl = slice(j * C_QPAD, (j + 1) * C_QPAD)
        vl = slice(j * V_SLOT, (j + 1) * V_SLOT)
        new = functools.partial(_kv_source, k_ref, lambda r: (0, r, kl), v_ref, lambda r: (0, r, vl))
        if not latent:
            return [new(SEG, own)]
        return [_kv_source(kc_ref, lambda r: (0, r, kl), vc_ref, lambda r: (0, r, vl), PAST_LEN),
                new(GROUP_TOKENS)]

    def emit(latent):
        heads = [(q[:, j * C_QPAD:(j + 1) * C_QPAD], sources_of(j, latent)) for j in range(MLA_PAIR)]
        o_ref[0] = jnp.concatenate(_softmax_pv_heads(heads, C_V), axis=-1).astype(BF16)

    _attn_branches(pl.program_id(0), emit)


def _mla_attention(q, k, v_aug, kc, vc):
    wq, wv = MLA_PAIR * C_QPAD, MLA_PAIR * V_SLOT
    cache = lambda g, p, c: (jnp.minimum(g, N_SAMPLE_GROUPS - 1), 0, p)
    return pl.pallas_call(
        _mla_kernel,
        grid=(N_GROUPS, C_HEADS // MLA_PAIR, SEGS_PER_GROUP),
        in_specs=[pl.BlockSpec((1, SEG, wq), lambda g, p, c: (g, c, p)),
                  pl.BlockSpec((1, GROUP_TOKENS, wq), lambda g, p, c: (g, 0, p)),
                  pl.BlockSpec((1, GROUP_TOKENS, wv), lambda g, p, c: (g, 0, p)),
                  pl.BlockSpec((1, PAST_LEN, wq), cache),
                  pl.BlockSpec((1, PAST_LEN, wv), cache)],
        out_specs=pl.BlockSpec((1, SEG, MLA_PAIR * C_V), lambda g, p, c: (g, c, p)),
        out_shape=jax.ShapeDtypeStruct((N_GROUPS, GROUP_TOKENS, C_HEADS * C_V), BF16),
        compiler_params=_cparams(3),
        name="mla_attention",
    )(q, k, v_aug, kc, vc)


def _out_kernel(*refs, even):
    if even:
        (xs_ref, xp_ref, att_ref, h_ref, gate_ref, g1_ref, w_ref, mnw_ref, n2w_ref, sc_ref, sh_ref, rwh_ref,
         rwl_ref, x1_ref, xpk_ref, aff_ref) = refs
        x = jnp.where(pl.program_id(0) >= N_SAMPLE_GROUPS, xp_ref[0], xs_ref[0])
        h = h_ref[0]
        mnw = mnw_ref[...]
        heads = [_rms(h[:, j * A_V_DIM:(j + 1) * A_V_DIM], mnw[:, j * A_V_DIM:(j + 1) * A_V_DIM])
                 for j in range(A_HEADS)]
        mix = jnp.concatenate(heads, axis=-1) * _sigmoid(gate_ref[0].astype(F32))
    else:
        (x_ref, att_ref, h_ref, gate_ref, g1_ref, w_ref, n2w_ref, sc_ref, sh_ref, rwh_ref, rwl_ref,
         x1_ref, xpk_ref, aff_ref) = refs
        x = x_ref[0]
        mix = h_ref[0] * jax.nn.gelu(gate_ref[0].astype(F32))
    half = w_ref.shape[0] // 2
    out = _dot(att_ref[0], w_ref[0:half, :]) + _dot(mix.astype(BF16), w_ref[half:, :])
    x1 = x + g1_ref[0] * out
    x1_ref[0] = x1
    xm = _rms(x1, n2w_ref[...]) * (1.0 + sc_ref[0]) + sh_ref[0]
    xpk_ref[0] = xm
    logits = _dot_x3(xm, rwh_ref[...], rwl_ref[...])
    lane = lax.broadcasted_iota(jnp.int32, logits.shape, 1)
    logits = jnp.where(lane < N_EXPERTS, logits, -jnp.inf)
    e = jnp.exp(logits - jnp.max(logits, axis=-1, keepdims=True))
    aff_ref[0] = e / jnp.sum(e, axis=-1, keepdims=True)


def _out_proj(x_args, att, h, gate, g1, w, mnw, n2w, sc, sh, rw_hi, rw_lo, *, even):
    tm = TOKEN_TILE
    if even:
        x_specs = list(_two_source_specs(tm))
        extra, extra_specs = [mnw], [_const_spec(mnw.shape)]
    else:
        x_specs = [_grp_spec(tm, D_MODEL)]
        extra, extra_specs = [], []
    widths = [(D_MODEL, F32), (D_MODEL, F32), (LANES, F32)]
    return pl.pallas_call(
        functools.partial(_out_kernel, even=even),
        grid=(N_GROUPS, GROUP_TOKENS // tm),
        in_specs=x_specs + [_grp_spec(tm, 512), _grp_spec(tm, 512), _grp_spec(tm, 512), _mod_spec(),
                            _const_spec(w.shape)] + extra_specs +
                 [_const_spec(n2w.shape), _mod_spec(), _mod_spec(), _const_spec(rw_hi.shape),
                  _const_spec(rw_lo.shape)],
        out_specs=[_grp_spec(tm, wd) for wd, _ in widths],
        out_shape=[jax.ShapeDtypeStruct((N_GROUPS, GROUP_TOKENS, wd), dt) for wd, dt in widths],
        compiler_params=_cparams(2),
        name="out_even" if even else "out_odd",
    )(*x_args, att, h, gate, g1, w, *extra, n2w, sc, sh, rw_hi, rw_lo)


def _scan_flags():
    g = pl.program_id(0)
    c = pl.program_id(1)
    cb = SEGS_PER_GROUP - 1 - c
    per_seq = jnp.where(g >= N_SAMPLE_GROUPS, 1, SEGS_PER_GROUP)
    starts = ((c % per_seq) == 0, (cb % per_seq) == per_seq - 1)
    ends = ((c % per_seq) == per_seq - 1, (cb % per_seq) == 0)
    return c, cb, starts, ends


def _store_or_add(ref, chunk, value, first_touch):
    rows = pl.ds(pl.multiple_of(chunk * SEG, SEG), SEG)

    @pl.when(first_touch)
    def _():
        ref[0, rows, :] = value

    @pl.when(jnp.logical_not(first_touch))
    def _():
        ref[0, rows, :] += value


def _log_sigmoid(x):
    return jnp.minimum(x, 0.0) - jnp.log1p(jnp.exp(-jnp.abs(x)))


C_AUG = 2 * A_V_DIM


def _mlstm_kernel(qf_ref, kf_ref, vf_ref, gf_ref, qb_ref, kb_ref, vb_ref, gb_ref, s0_ref, m0_ref,
                  h_ref, sf_ref, sb_ref, mf_ref, mb_ref, c_scr, m_scr):
    c, cb, starts, _ = _scan_flags()
    L = SEG
    row = lax.broadcasted_iota(jnp.int32, (L, L), 0)
    col = lax.broadcasted_iota(jnp.int32, (L, L), 1)
    lane = lax.broadcasted_iota(jnp.int32, (1, LANES), 1)
    ones_col = (lax.broadcasted_iota(jnp.int32, (L, A_V_DIM), 1) == 0).astype(BF16)

    for d in range(2):
        @pl.when(starts[d])
        def _():
            c_scr[d] = s0_ref[0, d]
            m_scr[d] = m0_ref[0, d]

    def run(d, q_ref, k_ref, v_ref, g_ref):
        mask = (col <= row) if d == 0 else (col >= row)
        tri = mask.astype(BF16)
        gates = g_ref[0]
        lf = _log_sigmoid(gates)
        hi, mid, lo = _split3(lf)
        cum = _dot(tri, hi) + (_dot(tri, mid) + _dot(tri, lo))
        total = jnp.sum(lf, axis=0, keepdims=True)
        gates_t = gates.T
        cum_t = cum.T
        q = q_ref[0]
        k = k_ref[0]
        v = v_ref[0]
        k_t = k.astype(F32).T
        m_vec = m_scr[d]
        m_out = m_vec
        hs = []
        for h in range(A_HEADS):
            ci, cf = d * 2 * A_HEADS + h, d * 2 * A_HEADS + A_HEADS + h
            bc, br = cum[:, cf:cf + 1], cum_t[cf:cf + 1, :]
            ir = gates_t[ci:ci + 1, :]
            m_prev = m_vec[:, h:h + 1]
            dmat = jnp.where(mask, bc - br + ir, -jnp.inf)
            m_inter = bc + m_prev
            m_t = jnp.maximum(m_inter, jnp.max(dmat, axis=-1, keepdims=True))
            w_inter = jnp.exp(m_inter - m_t)
            qh = q[:, h * A_QK_DIM:(h + 1) * A_QK_DIM]
            kh = k[:, h * A_QK_DIM:(h + 1) * A_QK_DIM]
            s = _dot_nt(qh, kh) * jnp.exp(dmat - m_t)
            state = c_scr[d, h]
            qc = _dot(qh, state.astype(BF16))
            v_aug = jnp.concatenate([v[:, h * A_V_DIM:(h + 1) * A_V_DIM], ones_col], axis=-1)
            sv = _dot(s.astype(BF16), v_aug)
            num = w_inter * qc[:, :A_V_DIM] + sv[:, :A_V_DIM]
            den = w_inter * qc[:, A_V_DIM:A_V_DIM + 1] + jnp.sum(s, axis=-1, keepdims=True)
            hs.append(num / jnp.maximum(jnp.abs(den), jnp.exp(-m_t)))
            tot = total[:, cf:cf + 1]
            g_row = tot - br + ir
            m_new = jnp.maximum(tot + m_prev, jnp.max(g_row, axis=-1, keepdims=True))
            decay = jnp.exp(tot + m_prev - m_new)
            kw = (k_t[h * A_QK_DIM:(h + 1) * A_QK_DIM, :] * jnp.exp(g_row - m_new)).astype(BF16)
            c_scr[d, h] = decay * state + _dot(kw, v_aug)
            m_out = jnp.where(lane == h, m_new, m_out)
        m_scr[d] = m_out
        return jnp.concatenate(hs, axis=-1)

    h_f = run(0, qf_ref, kf_ref, vf_ref, gf_ref)
    h_b = run(1, qb_ref, kb_ref, vb_ref, gb_ref)
    _store_or_add(h_ref, c, h_f, c < cb)
    _store_or_add(h_ref, cb, h_b, c < cb)
    sf_ref[0, 0] = c_scr[0]
    sb_ref[0, 0] = c_scr[1]
    mf_ref[0, 0] = m_scr[0]
    mb_ref[0, 0] = m_scr[1]


def _mlstm(qa, ka, va, gates, s0, m0):
    fwd = lambda w: pl.BlockSpec((1, SEG, w), lambda g, c: (g, c, 0))
    bwd = lambda w: pl.BlockSpec((1, SEG, w), lambda g, c: (g, SEGS_PER_GROUP - 1 - c, 0))
    st_shape = (1, 1, A_HEADS, A_QK_DIM, C_AUG)
    st_f = pl.BlockSpec(st_shape, lambda g, c: (g, c, 0, 0, 0))
    st_b = pl.BlockSpec(st_shape, lambda g, c: (g, SEGS_PER_GROUP - 1 - c, 0, 0, 0))
    m_f = pl.BlockSpec((1, 1, 1, LANES), lambda g, c: (g, c, 0, 0))
    m_b = pl.BlockSpec((1, 1, 1, LANES), lambda g, c: (g, SEGS_PER_GROUP - 1 - c, 0, 0))
    widths = (A_HEADS * A_QK_DIM, A_HEADS * A_QK_DIM, A_HEADS * A_V_DIM, LANES)
    st_out = jax.ShapeDtypeStruct((N_GROUPS, SEGS_PER_GROUP, A_HEADS, A_QK_DIM, C_AUG), F32)
    m_out = jax.ShapeDtypeStruct((N_GROUPS, SEGS_PER_GROUP, 1, LANES), F32)
    return pl.pallas_call(
        _mlstm_kernel,
        grid=(N_GROUPS, SEGS_PER_GROUP),
        in_specs=[fwd(w) for w in widths] + [bwd(w) for w in widths] + [
            pl.BlockSpec((1, 2, A_HEADS, A_QK_DIM, C_AUG), lambda g, c: (g, 0, 0, 0, 0)),
            pl.BlockSpec((1, 2, 1, LANES), lambda g, c: (g, 0, 0, 0))],
        out_specs=[pl.BlockSpec((1, GROUP_TOKENS, A_HEADS * A_V_DIM), lambda g, c: (g, 0, 0)),
                   st_f, st_b, m_f, m_b],
        out_shape=[jax.ShapeDtypeStruct((N_GROUPS, GROUP_TOKENS, A_HEADS * A_V_DIM), F32),
                   st_out, st_out, m_out, m_out],
        scratch_shapes=[pltpu.VMEM((2, A_HEADS, A_QK_DIM, C_AUG), F32), pltpu.VMEM((2, 1, LANES), F32)],
        compiler_params=_cparams(2),
        name="mlstm",
    )(qa, ka, va, gates, qa, ka, va, gates, s0, m0)


def _one_minus_square_of_exp(log_a):
    a = jnp.exp(log_a)
    return a, jnp.tanh(-log_a) * (1.0 + a * a)


def _linear_scan(a, b, h_in, reverse):
    n = a.shape[0] // SUBLANES
    a3 = a.reshape(n, SUBLANES, D_RNN)
    b3 = b.reshape(n, SUBLANES, D_RNN)
    row = lax.broadcasted_iota(jnp.int32, a3.shape, 1)
    for d in (1, 2, 4):
        shift = SUBLANES - d if reverse else d
        valid = (row < SUBLANES - d) if reverse else (row >= d)
        b3 = jnp.where(valid, a3 * pltpu.roll(b3, shift, 1) + b3, b3)
        a3 = jnp.where(valid, a3 * pltpu.roll(a3, shift, 1), a3)
    h = h_in
    outs = [None] * n
    for j in (reversed(range(n)) if reverse else range(n)):
        hj = a3[j] * h + b3[j]
        outs[j] = hj
        h = hj[0:1] if reverse else hj[SUBLANES - 1:SUBLANES]
    return jnp.concatenate(outs, axis=0), h


def _rglru_kernel(xf_ref, pf_ref, nf_ref, xb_ref, pb_ref, nb_ref, cw_ref, cb_ref, wa_ref, wx_ref, ba_ref,
                  bx_ref, lam_ref, h0_ref, o_ref, sf_ref, sb_ref, h_scr):
    c, cb, starts, ends = _scan_flags()
    cw = cw_ref[...]

    def run(d, x_ref, p_ref, n_ref, chunk, start, end, first_touch):
        @pl.when(start if d == 0 else end)
        def _():
            h_scr[d] = h0_ref[0, d]

        prev = jnp.where(start, 0.0, p_ref[0])
        nxt = jnp.where(end, 0.0, n_ref[0])
        xcat = jnp.concatenate([prev, x_ref[0], nxt], axis=0)
        xc = cb_ref[...]
        for j in range(CONV_W):
            off = SUBLANES - CONV_LEFT + j
            xc = xc + cw[j:j + 1, :] * xcat[off:off + SEG, :]
        xcb = xc.astype(BF16)
        r = _sigmoid(_dot(xcb, wa_ref[d]) + ba_ref[d])
        i = _sigmoid(_dot(xcb, wx_ref[d]) + bx_ref[d])
        lam = lam_ref[d]
        softplus_neg = jnp.maximum(-lam, 0.0) + jnp.log1p(jnp.exp(-jnp.abs(lam)))
        log_a = (-RG_C * r) * softplus_neg
        a, one_minus_a2 = _one_minus_square_of_exp(log_a)
        b = jnp.sqrt(one_minus_a2) * (i * xc)
        hs, h_last = _linear_scan(a, b, h_scr[d], reverse=(d == 1))
        h_scr[d] = h_last
        _store_or_add(o_ref, chunk, hs, first_touch)

    run(0, xf_ref, pf_ref, nf_ref, c, starts[0], ends[0], c < cb)
    run(1, xb_ref, pb_ref, nb_ref, cb, ends[1], starts[1], c < cb)
    sf_ref[0, 0] = h_scr[0]
    sb_ref[0, 0] = h_scr[1]


def _rglru(xr, conv_w, conv_b, wa, wx, ba, bx, lam, h0):
    blocks = SEG // SUBLANES
    n_blocks = GROUP_TOKENS // SUBLANES
    cbk = lambda c: SEGS_PER_GROUP - 1 - c
    x_f = pl.BlockSpec((1, SEG, D_RNN), lambda g, c: (g, c, 0))
    x_b = pl.BlockSpec((1, SEG, D_RNN), lambda g, c: (g, cbk(c), 0))
    halo = lambda f: pl.BlockSpec((1, SUBLANES, D_RNN), f)
    p_f = halo(lambda g, c: (g, jnp.maximum(c * blocks - 1, 0), 0))
    n_f = halo(lambda g, c: (g, jnp.minimum((c + 1) * blocks, n_blocks - 1), 0))
    p_b = halo(lambda g, c: (g, jnp.maximum(cbk(c) * blocks - 1, 0), 0))
    n_b = halo(lambda g, c: (g, jnp.minimum((cbk(c) + 1) * blocks, n_blocks - 1), 0))
    st = jax.ShapeDtypeStruct((N_GROUPS, SEGS_PER_GROUP, 1, D_RNN), F32)
    return pl.pallas_call(
        _rglru_kernel,
        grid=(N_GROUPS, SEGS_PER_GROUP),
        in_specs=[x_f, p_f, n_f, x_b, p_b, n_b, _const_spec(conv_w.shape), _const_spec(conv_b.shape),
                  _const_spec(wa.shape), _const_spec(wx.shape), _const_spec(ba.shape), _const_spec(bx.shape),
                  _const_spec(lam.shape), pl.BlockSpec((1, 2, 1, D_RNN), lambda g, c: (g, 0, 0, 0))],
        out_specs=[pl.BlockSpec((1, GROUP_TOKENS, D_RNN), lambda g, c: (g, 0, 0)),
                   pl.BlockSpec((1, 1, 1, D_RNN), lambda g, c: (g, c, 0, 0)),
                   pl.BlockSpec((1, 1, 1, D_RNN), lambda g, c: (g, cbk(c), 0, 0))],
        out_shape=[jax.ShapeDtypeStruct((N_GROUPS, GROUP_TOKENS, D_RNN), F32), st, st],
        scratch_shapes=[pltpu.VMEM((2, 1, D_RNN), F32)],
        compiler_params=_cparams(2),
        name="rglru",
    )(xr, xr, xr, xr, xr, xr, conv_w, conv_b, wa, wx, ba, bx, lam, h0)


OD_COLS = dict(cq=(0, 384), ckv=(384, 640), xr=(640, 1152), xg=(1152, 1664))
MLA_SCALE = (C_NOPE + C_ROPE) ** -0.5 * LOG2E


def _mla_keys(ckv_b, kr_b, wuk_ref, place_ref, wuv_ref):
    kcat = _dot(ckv_b, wuk_ref[...]) + _dot(kr_b, place_ref[...])
    v = _dot(ckv_b, wuv_ref[...])
    lane = lax.broadcasted_iota(jnp.int32, (1, v.shape[-1]), 1)
    return kcat.astype(BF16), (v + (lane % V_SLOT == C_V).astype(F32)).astype(BF16)


def _in_odd_kernel(x_ref, moe_ref, g2_ref, nw_ref, sc_ref, sh_ref, w_ref, wkr_ref, qnw_ref, kvnw_ref, wuq_ref,
                   wuk_ref, place_ref, wuv_ref, cq_ref, sq_ref, ck_ref, sk_ref,
                   x2_ref, q_ref, kcat_ref, v_ref, ckv_ref, kr_ref, xr_ref, xg_ref):
    x = x_ref[0] + g2_ref[0] * moe_ref[0]
    x2_ref[0] = x
    xm = _rms(x, nw_ref[...]) * (1.0 + sc_ref[0]) + sh_ref[0]
    xb = xm.astype(BF16)

    def proj(name):
        lo, hi = OD_COLS[name]
        return _dot(xb, w_ref[:, lo:hi])

    cq = _rms(proj("cq"), qnw_ref[...])
    q = _dot(cq.astype(BF16), wuq_ref[...])
    q = _rope(q, _tile_lanes(cq_ref[0], C_HEADS), _tile_lanes(sq_ref[0], C_HEADS), C_ROPE // 4)
    q_ref[0] = (q * MLA_SCALE).astype(BF16)
    ckv = _rms(proj("ckv"), kvnw_ref[...])
    ckv_ref[0] = ckv
    kr = _rope(_dot(xb, wkr_ref[...]), ck_ref[0], sk_ref[0], C_ROPE // 4)
    kr_ref[0] = kr
    kcat_ref[0], v_ref[0] = _mla_keys(ckv.astype(BF16), kr.astype(BF16), wuk_ref, place_ref, wuv_ref)
    xr_ref[0] = proj("xr")
    xg_ref[0] = proj("xg").astype(BF16)


def _in_odd(x, moe, g2, nw, sc, sh, w_main, w_kr, qnw, kvnw, w_uq, w_uk, place, w_uv, cos_q, sin_q, cos_k, sin_k):
    tm = TOKEN_TILE
    widths = [(D_MODEL, F32), (C_HEADS * C_QPAD, BF16), (C_HEADS * C_QPAD, BF16), (C_HEADS * V_SLOT, BF16),
              (C_KV_LORA, F32), (LANES, F32), (D_RNN, F32), (D_RNN, BF16)]
    consts = [nw, None, None, w_main, w_kr, qnw, kvnw, w_uq, w_uk, place, w_uv]
    const_specs = [_mod_spec() if a is None else _const_spec(a.shape) for a in consts]
    return pl.pallas_call(
        _in_odd_kernel,
        grid=(N_GROUPS, GROUP_TOKENS // tm),
        in_specs=[_grp_spec(tm, D_MODEL), _grp_spec(tm, D_MODEL), _mod_spec()] + const_specs +
                 [_rope_spec(tm)] * 4,
        out_specs=[_grp_spec(tm, w) for w, _ in widths],
        out_shape=[jax.ShapeDtypeStruct((N_GROUPS, GROUP_TOKENS, w), dt) for w, dt in widths],
        compiler_params=_cparams(2),
        name="in_odd",
    )(x, moe, g2, nw, sc, sh, w_main, w_kr, qnw, kvnw, w_uq, w_uk, place, w_uv, cos_q, sin_q, cos_k, sin_k)


def _kv_expand_kernel(ckv_ref, kr_ref, wuk_ref, place_ref, wuv_ref, kcat_ref, v_ref):
    kcat_ref[0], v_ref[0] = _mla_keys(ckv_ref[0].astype(BF16), kr_ref[0].astype(BF16), wuk_ref, place_ref, wuv_ref)


def _kv_expand(ckv, kr_pad, w_uk, place, w_uv):
    n = ckv.shape[0]
    cst = lambda a: pl.BlockSpec(a.shape, lambda b: (0,) * a.ndim)
    return pl.pallas_call(
        _kv_expand_kernel,
        grid=(n,),
        in_specs=[pl.BlockSpec((1, PAST_LEN, C_KV_LORA), lambda b: (b, 0, 0)),
                  pl.BlockSpec((1, PAST_LEN, LANES), lambda b: (b, 0, 0)), cst(w_uk), cst(place), cst(w_uv)],
        out_specs=[pl.BlockSpec((1, PAST_LEN, C_HEADS * C_QPAD), lambda b: (b, 0, 0)),
                   pl.BlockSpec((1, PAST_LEN, C_HEADS * V_SLOT), lambda b: (b, 0, 0))],
        out_shape=[jax.ShapeDtypeStruct((n, PAST_LEN, C_HEADS * C_QPAD), BF16),
                   jax.ShapeDtypeStruct((n, PAST_LEN, C_HEADS * V_SLOT), BF16)],
        compiler_params=_cparams(1),
        name="kv_expand",
    )(ckv, kr_pad, w_uk, place, w_uv)


def _route_kernel(aff_ref, idx_ref, cum_scr):
    is_ctx = pl.program_id(0) >= N_SAMPLE_GROUPS
    cap = jnp.where(is_ctx, CTX_CAP, SLOTS).astype(F32)
    aff = aff_ref[0].reshape(SEGS_PER_GROUP, SEG, LANES)

    def per_set(per_seg, combine):
        whole = jnp.broadcast_to(combine(per_seg, axis=0, keepdims=True), per_seg.shape)
        return jnp.where(is_ctx, per_seg, whole)

    def count(mask):
        return per_set(jnp.sum(mask.astype(F32), axis=1), jnp.sum)

    def as_float(word):
        return lax.bitcast_convert_type(word, F32)[:, None, :]

    def search(i, word):
        cand = word | lax.shift_left(jnp.int32(1), 30 - i)
        return jnp.where(count(aff >= as_float(cand)) >= cap, cand, word)

    word = lax.fori_loop(0, 31, search, jnp.zeros((SEGS_PER_GROUP, LANES), jnp.int32))
    upper = as_float(word + 1)
    kth = as_float(word)
    left = cap - count(aff >= upper)
    found = jnp.zeros(left.shape, jnp.bool_)
    for _ in range(3):
        cand = per_set(jnp.max(jnp.where(aff < upper, aff, -1.0), axis=1), jnp.max)[:, None, :]
        n_cand = count(aff == cand)
        hit = jnp.logical_and(jnp.logical_not(found), left <= n_cand)
        kth = jnp.where(hit[:, None, :], cand, kth)
        found = jnp.logical_or(found, hit)
        left = jnp.where(found, left, left - n_cand)
        upper = jnp.where(found[:, None, :], upper, cand)
    above = aff > kth
    tied = aff == kth
    need = cap - count(above)

    tri = (lax.broadcasted_iota(jnp.int32, (SEG, SEG), 1) <= lax.broadcasted_iota(jnp.int32, (SEG, SEG), 0)).astype(BF16)

    def prefix(mask, across_segments):
        m = mask.astype(BF16)
        outs = []
        offset = jnp.zeros((1, LANES), F32)
        for s in range(SEGS_PER_GROUP):
            p = _dot(tri, m[s])
            outs.append(p + jnp.where(across_segments, offset, 0.0))
            offset = offset + p[SEG - 1:SEG, :]
        return jnp.stack(outs)

    tied_rank = prefix(tied, jnp.logical_not(is_ctx)) - tied.astype(F32)
    keep = above | (tied & (tied_rank < need[:, None, :]))
    cum_scr[...] = prefix(keep, True)

    slot = lax.broadcasted_iota(jnp.int32, (1, SLOTS), 1).astype(F32)
    for e in range(N_EXPERTS):
        def block(s, acc):
            ce = cum_scr[s][:, e:e + 1]
            return acc + jnp.sum((ce <= slot).astype(F32), axis=0, keepdims=True)
        n_before = lax.fori_loop(0, SEGS_PER_GROUP, block, jnp.zeros((1, SLOTS), F32))
        idx_ref[0, e:e + 1, :] = jnp.minimum(n_before, GROUP_TOKENS - 1.0).astype(jnp.int32)


def _route(aff):
    return pl.pallas_call(
        _route_kernel,
        grid=(N_GROUPS,),
        in_specs=[pl.BlockSpec((1, GROUP_TOKENS, LANES), lambda g: (g, 0, 0))],
        out_specs=pl.BlockSpec((1, N_EXPERTS, SLOTS), lambda g: (g, 0, 0)),
        out_shape=jax.ShapeDtypeStruct((N_GROUPS, N_EXPERTS, SLOTS), jnp.int32),
        scratch_shapes=[pltpu.VMEM((SEGS_PER_GROUP, SEG, LANES), F32)],
        compiler_params=_cparams(1),
        name="route",
    )(aff)


SCATTER_BATCH = 8


def _moe_kernel(idx_ref, xpk_ref, aff_ref, wg_ref, wu_ref, wd_ref, o_ref, xe_scr, ga_scr, ye_scr):
    g = pl.program_id(0)
    e = pl.program_id(1)
    base = (g * N_EXPERTS + e) * SLOTS

    @pl.when(e == 0)
    def _():
        o_ref[...] = jnp.zeros_like(o_ref)

    def gather(s, carry):
        t = idx_ref[base + s]
        xe_scr[pl.ds(s, 1), :] = xpk_ref[0, pl.ds(t, 1), :]
        ga_scr[pl.ds(s, 1), :] = aff_ref[0, pl.ds(t, 1), :]
        return carry

    lax.fori_loop(0, SLOTS, gather, 0, unroll=8)
    xe = xe_scr[...].astype(BF16)
    lane = lax.broadcasted_iota(jnp.int32, (SLOTS, LANES), 1)
    gate = jnp.sum(jnp.where(lane == e, ga_scr[...], 0.0), axis=-1, keepdims=True)
    a = _dot(xe, wg_ref[0])
    hid = (a * _sigmoid(a)) * _dot(xe, wu_ref[0])
    ye_scr[...] = _dot(hid.astype(BF16), wd_ref[0]) * gate

    def scatter(j, carry):
        s0 = j * SCATTER_BATCH
        toks = [idx_ref[base + s0 + k] for k in range(SCATTER_BATCH)]
        rows = [o_ref[0, pl.ds(t, 1), :] + ye_scr[pl.ds(s0 + k, 1), :] for k, t in enumerate(toks)]
        for t, row in zip(toks, rows):
            o_ref[0, pl.ds(t, 1), :] = row
        return carry

    lax.fori_loop(0, SLOTS // SCATTER_BATCH, scatter, 0)


def _moe(idx, xpk, aff, w_gate, w_up, w_down):
    once = pl.Buffered(1)
    return pl.pallas_call(
        _moe_kernel,
        grid_spec=pltpu.PrefetchScalarGridSpec(
            num_scalar_prefetch=1,
            grid=(N_GROUPS, N_EXPERTS),
            in_specs=[
                pl.BlockSpec((1, GROUP_TOKENS, D_MODEL), lambda g, e, idx: (g, 0, 0), pipeline_mode=once),
                pl.BlockSpec((1, GROUP_TOKENS, LANES), lambda g, e, idx: (g, 0, 0), pipeline_mode=once),
                pl.BlockSpec((1, D_MODEL, EXPERT_FF), lambda g, e, idx: (e, 0, 0)),
                pl.BlockSpec((1, D_MODEL, EXPERT_FF), lambda g, e, idx: (e, 0, 0)),
                pl.BlockSpec((1, EXPERT_FF, D_MODEL), lambda g, e, idx: (e, 0, 0)),
            ],
            out_specs=pl.BlockSpec((1, GROUP_TOKENS, D_MODEL), lambda g, e, idx: (g, 0, 0), pipeline_mode=once),
            scratch_shapes=[pltpu.VMEM((SLOTS, D_MODEL), F32), pltpu.VMEM((SLOTS, LANES), F32),
                            pltpu.VMEM((SLOTS, D_MODEL), F32)],
        ),
        out_shape=jax.ShapeDtypeStruct((N_GROUPS, GROUP_TOKENS, D_MODEL), F32),
        compiler_params=_cparams(2),
        name="moe",
    )(idx.reshape(-1), xpk, aff, w_gate, w_up, w_down)


def _final_kernel(x_ref, moe_ref, g2_ref, nw_ref, o_ref):
    o_ref[0] = _rms(x_ref[0] + g2_ref[0] * moe_ref[0], nw_ref[...])


def _final(x, moe, g2, nw, first_group, n_groups):
    tm = TOKEN_TILE
    src = lambda w: pl.BlockSpec((1, tm, w), lambda g, i: (g + first_group, i, 0))
    return pl.pallas_call(
        _final_kernel,
        grid=(n_groups, GROUP_TOKENS // tm),
        in_specs=[src(D_MODEL), src(D_MODEL), pl.BlockSpec((1, 1, D_MODEL), lambda g, i: (g + first_group, 0, 0)),
                  _const_spec(nw.shape)],
        out_specs=_grp_spec(tm, D_MODEL),
        out_shape=jax.ShapeDtypeStruct((n_groups, GROUP_TOKENS, D_MODEL), F32),
        compiler_params=_cparams(2),
        name="final_norm",
    )(x, moe, g2, nw)


def _hi_lo(w):
    hi = w.astype(BF16)
    return hi, (w - hi.astype(F32)).astype(BF16)


def _pad_cols(w, width=LANES):
    return jnp.pad(w, ((0, 0),) * (w.ndim - 1) + ((0, width - w.shape[-1]),))


def _ctx_only(a):
    tail = a[N_SAMPLE_GROUPS:]
    return tail.reshape((N_CTX_GROUPS * SEGS_PER_GROUP,) + tail.shape[2:])


def _with_ctx_zeros(a):
    return jnp.concatenate([a, jnp.zeros((N_CTX_GROUPS,) + a.shape[1:], a.dtype)], axis=0)


def _moe_block(xpk, aff, w_gate, w_up, w_down):
    idx = _route(aff)
    return _moe(idx, xpk, aff, w_gate.astype(BF16), w_up.astype(BF16), w_down.astype(BF16))


def kernel(x_prompt, x_sample, state_mlstm_C, state_mlstm_n, state_mlstm_m, cache_gqa_k, cache_gqa_v, cache_mla_ckv, cache_mla_krope, state_rglru_h, c, c_ctx, norm1_w, norm2_w, final_norm_w, ada_w, ada_b, ev_w_in, ev_w_out, gqa_q_norm, gqa_k_norm, mlstm_gate_b, mlstm_norm_w, od_w_in, od_w_out, mla_q_norm, mla_kv_norm, mla_w_uq, mla_w_ukv, rg_conv_w, rg_conv_b, rg_wa, rg_ba, rg_wx, rg_bx, rg_lambda, router_w, exp_w_gate, exp_w_up, exp_w_down):
    n_ctx = x_prompt.shape[0]
    xs = x_sample
    xp = x_prompt.reshape(N_CTX_GROUPS, GROUP_TOKENS, D_MODEL)

    cond = jnp.concatenate([c, jnp.broadcast_to(c_ctx[None, :], (16 - N_SAMPLE_GROUPS, D_MODEL))], axis=0)
    mods = _adaln(cond, ada_w, ada_b)[:, :N_GROUPS]
    mod = [[m[:, None, :] for m in jnp.split(mods[layer], 6, axis=-1)] for layer in range(2)]
    router = [_hi_lo(_pad_cols(router_w[layer])) for layer in range(2)]

    sh1, sc1, g1, sh2, sc2, g2 = mod[0]
    w_in = ev_w_in[0]
    wg_hi, wg_lo = _hi_lo(_pad_cols(w_in[:, EV_MAIN:]))
    bd = jnp.asarray(np.kron(np.eye(B_HEADS), np.full((B_HEAD_DIM, B_HEAD_DIM), 1.0 / B_HEAD_DIM)), BF16)
    cos_b, sin_b = _rope_tables(B_HEAD_DIM)
    cos_b, sin_b = _pad_tables(np.tile(cos_b, (1, 2)), np.tile(sin_b, (1, 2)))
    v_lo, v_hi = EV_COLS["v"]
    w_vaug = jnp.pad(w_in[:, v_lo:v_hi].reshape(D_MODEL, B_KV_HEADS, B_HEAD_DIM),
                     ((0, 0), (0, 0), (0, V_SLOT - B_HEAD_DIM))).reshape(D_MODEL, -1).astype(BF16)
    slot_one = np.zeros((1, B_KV_HEADS, V_SLOT), np.float32)
    slot_one[:, :, B_HEAD_DIM] = 1.0
    q, kb, vb, qa, ka, va, oa, ga, k0, k1, v_aug = _in_even(
        xs, xp, norm1_w[0][None, :], sc1, sh1, w_in[:, :EV_MAIN].astype(BF16), wg_hi, wg_lo,
        _pad_cols(mlstm_gate_b[0][None, :]), bd, jnp.tile(gqa_q_norm[0], B_HEADS)[None, :],
        jnp.tile(gqa_k_norm[0], B_KV_HEADS)[None, :], cos_b, sin_b, w_vaug,
        jnp.asarray(slot_one.reshape(1, -1)))
    kc = jnp.swapaxes(cache_gqa_k[:, 0], 1, 2).astype(BF16)
    vc = jnp.pad(cache_gqa_v[:, 0], ((0, 0), (0, 0), (0, 0), (0, V_SLOT - B_HEAD_DIM))) + jnp.asarray(slot_one)
    att = _gqa_attention(q, k0, k1, v_aug, kc, vc.reshape(N_SAMPLE_GROUPS, PAST_LEN, -1).astype(BF16))
    s0 = jnp.concatenate([state_mlstm_C[:, 0], state_mlstm_n[:, 0][..., None],
                          jnp.zeros(state_mlstm_C[:, 0].shape[:-1] + (C_AUG - A_V_DIM - 1,), F32)], axis=-1)
    m0 = _pad_cols(state_mlstm_m[:, 0])[:, :, None, :]
    h_a, st_f, st_b, m_f, m_b = _mlstm(qa, ka, va, ga, _with_ctx_zeros(s0), _with_ctx_zeros(m0))
    x1, xpk, aff = _out_proj((xs, xp), att, h_a, oa, g1, ev_w_out[0].astype(BF16), mlstm_norm_w[0][None, :],
                             norm2_w[0][None, :], sc2, sh2, *router[0], even=True)
    moe0 = _moe_block(xpk, aff, exp_w_gate[0], exp_w_up[0], exp_w_down[0])
    g2_0 = g2

    sh1, sc1, g1, sh2, sc2, g2 = mod[1]
    w_in = od_w_in[0]
    o_kr = C_Q_LORA + C_KV_LORA
    w_main = jnp.concatenate([w_in[:, :o_kr], w_in[:, o_kr + C_ROPE:]], axis=1).astype(BF16)
    w_kr = _pad_cols(w_in[:, o_kr:o_kr + C_ROPE]).astype(BF16)
    uq = mla_w_uq[0].reshape(C_Q_LORA, C_HEADS, C_NOPE + C_ROPE)
    w_uq = jnp.pad(uq, ((0, 0), (0, 0), (0, C_QPAD - C_NOPE - C_ROPE))).reshape(C_Q_LORA, -1).astype(BF16)
    ukv = mla_w_ukv[0].reshape(C_KV_LORA, C_HEADS, C_NOPE + C_V)
    w_uk = jnp.pad(ukv[..., :C_NOPE], ((0, 0), (0, 0), (0, C_QPAD - C_NOPE))).reshape(C_KV_LORA, -1).astype(BF16)
    w_uv = jnp.pad(ukv[..., C_NOPE:], ((0, 0), (0, 0), (0, V_SLOT - C_V))).reshape(C_KV_LORA, -1).astype(BF16)
    place_np = np.zeros((LANES, C_HEADS, C_QPAD), np.float32)
    for j in range(C_ROPE):
        place_np[j, :, C_NOPE + j] = 1.0
    place = jnp.asarray(place_np.reshape(LANES, -1), BF16)
    cos_c, sin_c = _rope_tables(C_ROPE)
    cos_k, sin_k = _pad_tables(cos_c, sin_c)
    cos_q = np.ones((GROUP_TOKENS, C_NOPE + C_ROPE), np.float32)
    sin_q = np.zeros((GROUP_TOKENS, C_NOPE + C_ROPE), np.float32)
    cos_q[:, C_NOPE:] = cos_c
    sin_q[:, C_NOPE:] = sin_c
    cos_q, sin_q = _pad_tables(cos_q, sin_q)
    x2, q, kcat, v, ckv, kr, xr, xg = _in_odd(
        x1, moe0, g2_0, norm1_w[1][None, :], sc1, sh1, w_main, w_kr, mla_q_norm[0][None, :],
        mla_kv_norm[0][None, :], w_uq, w_uk, place, w_uv, cos_q, sin_q, cos_k, sin_k)
    kcat_c, v_c = _kv_expand(cache_mla_ckv[:, 0], _pad_cols(cache_mla_krope[:, 0].reshape(-1, C_ROPE)).reshape(
        N_SAMPLE_GROUPS, PAST_LEN, LANES), w_uk, place, w_uv)
    att = _mla_attention(q, kcat, v, kcat_c, v_c)
    eye = jnp.eye(RG_BLOCKS, dtype=F32)
    dense = lambda w: jnp.einsum("knde,nm->kndme", w, eye).reshape(2, D_RNN, D_RNN).astype(BF16)
    h0 = _with_ctx_zeros(state_rglru_h[:, 0][:, :, None, :])
    rg, rs_f, rs_b = _rglru(xr, rg_conv_w[0], rg_conv_b[0][None, :], dense(rg_wa[0]), dense(rg_wx[0]),
                            rg_ba[0][:, None, :], rg_bx[0][:, None, :], rg_lambda[0][:, None, :], h0)
    x3, xpk, aff = _out_proj((x2,), att, rg, xg, g1, od_w_out[0].astype(BF16), None, norm2_w[1][None, :],
                             sc2, sh2, *router[1], even=False)
    moe1 = _moe_block(xpk, aff, exp_w_gate[1], exp_w_up[1], exp_w_down[1])

    fnw = final_norm_w[None, :]
    y_sample = _final(x3, moe1, g2, fnw, 0, N_SAMPLE_GROUPS)
    y_prompt = _final(x3, moe1, g2, fnw, N_SAMPLE_GROUPS, N_CTX_GROUPS).reshape(n_ctx, SEG, D_MODEL)

    st = jnp.stack([_ctx_only(st_f), _ctx_only(st_b)], axis=1)
    new_c = st[..., :A_V_DIM][:, None]
    new_n = st[..., A_V_DIM][:, None]
    new_m = jnp.stack([_ctx_only(m_f)[:, 0, :A_HEADS], _ctx_only(m_b)[:, 0, :A_HEADS]], axis=1)[:, None]
    new_gk = kb[N_SAMPLE_GROUPS:].reshape(n_ctx, 1, SEG, B_KV_HEADS, B_HEAD_DIM)
    new_gv = vb[N_SAMPLE_GROUPS:].reshape(n_ctx, 1, SEG, B_KV_HEADS, B_HEAD_DIM)
    new_ckv = ckv[N_SAMPLE_GROUPS:].reshape(n_ctx, 1, SEG, C_KV_LORA)
    new_kr = kr[N_SAMPLE_GROUPS:, :, :C_ROPE].reshape(n_ctx, 1, SEG, C_ROPE)
    new_rh = jnp.stack([_ctx_only(rs_f)[:, 0], _ctx_only(rs_b)[:, 0]], axis=1)[:, None]
    return (y_prompt, y_sample, new_c, new_n, new_m, new_gk, new_gv, new_ckv, new_kr, new_rh)
```

```python
import functools

import numpy as np
import jax
import jax.numpy as jnp
from jax import lax
from jax.experimental import pallas as pl
from jax.experimental.pallas import tpu as pltpu

F32 = jnp.float32
BF16 = jnp.bfloat16

D_MODEL = 1024
N_SAMPLE_GROUPS = 8
N_CTX_GROUPS = 2
N_GROUPS = N_SAMPLE_GROUPS + N_CTX_GROUPS
GROUP_TOKENS = 4096
SEG = 256
SEGS_PER_GROUP = GROUP_TOKENS // SEG
PAST_LEN = 512
GRID_W = 64
EPS = 1e-6
ROPE_BASE = 10000.0

B_HEADS, B_KV_HEADS, B_HEAD_DIM = 8, 2, 64
A_HEADS, A_QK_DIM, A_V_DIM = 4, 64, 128
C_HEADS, C_Q_LORA, C_KV_LORA, C_NOPE, C_ROPE, C_V = 8, 384, 256, 64, 32, 64
C_QPAD = 128
D_RNN, RG_BLOCKS, RG_C, CONV_W, CONV_LEFT = 512, 8, 8.0, 4, 2
N_EXPERTS, EXPERT_FF = 16, 512
SLOTS = 512
CTX_CAP = 2 * SEG // N_EXPERTS
LANES = 128
SUBLANES = 8
TOKEN_TILE = 512
VMEM_LIMIT = 56 * 1024 * 1024


def _cparams(n_axes, vmem=VMEM_LIMIT):
    return pltpu.CompilerParams(dimension_semantics=("arbitrary",) * n_axes, vmem_limit_bytes=vmem)


def _dot(a, b):
    return jnp.dot(a, b, preferred_element_type=F32)


def _dot_nt(a, b):
    return lax.dot_general(a, b, (((1,), (1,)), ((), ())), preferred_element_type=F32)


def _split3(a):
    hi = a.astype(BF16)
    r = a - hi.astype(F32)
    mid = r.astype(BF16)
    lo = (r - mid.astype(F32)).astype(BF16)
    return hi, mid, lo


def _dot_x3(a, w_hi, w_lo):
    a_hi = a.astype(BF16)
    a_lo = (a - a_hi.astype(F32)).astype(BF16)
    return _dot(a_hi, w_hi) + (_dot(a_lo, w_hi) + _dot(a_hi, w_lo))


def _rms(x, w):
    return (x * lax.rsqrt(jnp.mean(x * x, axis=-1, keepdims=True) + EPS)) * w


def _sigmoid(x):
    return 1.0 / (1.0 + jnp.exp(-x))


def _rope(x, cos, sin_signed, half):
    n = x.shape[-1]
    lane = lax.broadcasted_iota(jnp.int32, x.shape, x.ndim - 1)
    first = (lane % (2 * half)) < half
    partner = jnp.where(first, pltpu.roll(x, n - half, x.ndim - 1), pltpu.roll(x, half, x.ndim - 1))
    return x * cos + partner * sin_signed


def _tile_lanes(a, reps):
    return a if reps == 1 else jnp.concatenate([a] * reps, axis=-1)


def _adaln_kernel(c_ref, w_ref, b_ref, o_ref):
    c = c_ref[...]
    a = c * _sigmoid(c)
    w = w_ref[0]
    w_hi = w.astype(BF16)
    w_lo = (w - w_hi.astype(F32)).astype(BF16)
    o_ref[0] = _dot_x3(a, w_hi, w_lo) + b_ref[0]


def _adaln(cond, ada_w, ada_b):
    depth, _, n6 = ada_w.shape
    rows = cond.shape[0]
    tn = 1024
    return pl.pallas_call(
        _adaln_kernel,
        grid=(depth, n6 // tn),
        in_specs=[
            pl.BlockSpec((rows, D_MODEL), lambda l, j: (0, 0)),
            pl.BlockSpec((1, D_MODEL, tn), lambda l, j: (l, 0, j)),
            pl.BlockSpec((1, 1, tn), lambda l, j: (l, 0, j)),
        ],
        out_specs=pl.BlockSpec((1, rows, tn), lambda l, j: (l, 0, j)),
        out_shape=jax.ShapeDtypeStruct((depth, rows, n6), F32),
        compiler_params=_cparams(2),
        name="adaln",
    )(cond, ada_w, ada_b.reshape(depth, 1, n6))


def _grp_spec(tm, width):
    return pl.BlockSpec((1, tm, width), lambda g, i: (g, i, 0))


ROW_TILE = (D_MODEL // LANES, LANES)


def _row_tiles(x):
    return x.reshape((x.shape[0],) + ROW_TILE)


def _row_tile_spec(tm, first_group=0):
    return pl.BlockSpec((1, tm) + ROW_TILE, lambda g, i: (g + first_group, i, 0, 0))


def _mod_spec():
    return pl.BlockSpec((1, 1, D_MODEL), lambda g, i: (g, 0, 0))


def _const_spec(shape):
    nd = len(shape)
    return pl.BlockSpec(shape, lambda g, i: (0,) * nd)


def _two_source_specs(tm):
    last = GROUP_TOKENS // tm - 1
    xs = pl.BlockSpec((1, tm, D_MODEL), lambda g, i: (jnp.minimum(g, N_SAMPLE_GROUPS - 1),
                                                      jnp.where(g < N_SAMPLE_GROUPS, i, last), 0))
    xp = pl.BlockSpec((1, tm, D_MODEL), lambda g, i: (jnp.maximum(g - N_SAMPLE_GROUPS, 0),
                                                      jnp.where(g < N_SAMPLE_GROUPS, 0, i), 0))
    return xs, xp


def _rope_spec(tm):
    return pl.BlockSpec((1, tm, LANES), lambda g, i: (jnp.where(g < N_SAMPLE_GROUPS, 0, 1), i, 0))


def _rope_tables(d):
    half = d // 4
    t = np.arange(GROUP_TOKENS)
    pos = np.stack([(t // GRID_W).astype(np.float32), (t % GRID_W).astype(np.float32)], axis=1)
    lane = np.arange(d)
    axis = lane // (d // 2)
    j = lane % half
    first = (lane % (d // 2)) < half
    inv_freq = (np.float32(ROPE_BASE) ** (-(np.arange(half, dtype=np.float32)) / np.float32(half))).astype(np.float32)
    ang = (pos[:, axis] * inv_freq[j][None, :]).astype(np.float32)
    cos = np.cos(ang.astype(np.float64)).astype(np.float32)
    sin = np.sin(ang.astype(np.float64)).astype(np.float32)
    sin = np.where(first[None, :], -sin, sin)
    return cos, sin


def _pad_tables(cos, sin):
    t, w = cos.shape
    cos_p = np.ones((t, LANES), np.float32)
    sin_p = np.zeros((t, LANES), np.float32)
    cos_p[:, :w] = cos
    sin_p[:, :w] = sin
    cos2 = np.stack([cos_p, np.ones_like(cos_p)])
    sin2 = np.stack([sin_p, np.zeros_like(sin_p)])
    return jnp.asarray(cos2), jnp.asarray(sin2)


EV_COLS = dict(q=(0, 512), k=(512, 640), v=(640, 768), qa=(768, 1024), ka=(1024, 1280), va=(1280, 1792),
               oa=(1792, 2304))
EV_MAIN = 2304


LOG2E = 1.4426950408889634
V_SLOT = LANES


def _in_even_kernel(xs_ref, xp_ref, nw_ref, sc_ref, sh_ref, w_ref, wgh_ref, wgl_ref, gb_ref, bd_ref, qnw_ref,
                    knw_ref, cos_ref, sin_ref, wv_ref, vone_ref,
                    q_ref, kb_ref, vb_ref, qa_ref, ka_ref, va_ref, oa_ref, ga_ref, k0_ref, k1_ref, vaug_ref):
    g = pl.program_id(0)
    x = jnp.where(g >= N_SAMPLE_GROUPS, xp_ref[0], xs_ref[0])
    xm = _rms(x, nw_ref[...]) * (1.0 + sc_ref[0]) + sh_ref[0]
    xb = xm.astype(BF16)
    cos = cos_ref[0]
    sin = sin_ref[0]

    def proj(name):
        lo, hi = EV_COLS[name]
        return _dot(xb, w_ref[:, lo:hi])

    q = proj("q")
    q = q * lax.rsqrt(_dot((q * q).astype(BF16), bd_ref[...]) + EPS) * qnw_ref[...]
    q = _rope(q, _tile_lanes(cos, 4), _tile_lanes(sin, 4), B_HEAD_DIM // 4)
    q_ref[0] = (q * (B_HEAD_DIM ** -0.5 * LOG2E)).astype(BF16)
    k = proj("k")
    k = k * lax.rsqrt(_dot((k * k).astype(BF16), bd_ref[0:LANES, 0:LANES]) + EPS) * knw_ref[...]
    k = _rope(k, cos, sin, B_HEAD_DIM // 4)
    kb_ref[0] = k
    k0_ref[0] = k[:, :B_HEAD_DIM].astype(BF16)
    k1_ref[0] = k[:, B_HEAD_DIM:].astype(BF16)
    vb_ref[0] = proj("v")
    vaug_ref[0] = (_dot(xb, wv_ref[...]) + vone_ref[...]).astype(BF16)
    qa_ref[0] = proj("qa").astype(BF16)
    ka_ref[0] = (proj("ka") * (A_QK_DIM ** -0.5)).astype(BF16)
    va_ref[0] = proj("va").astype(BF16)
    oa_ref[0] = proj("oa").astype(BF16)
    ga_ref[0] = _dot_x3(xm, wgh_ref[...], wgl_ref[...]) + gb_ref[...]


def _in_even(xs, xp, nw, sc, sh, w_main, wg_hi, wg_lo, gb, bd, qnw, knw, cos, sin, w_vaug, v_one):
    tm = TOKEN_TILE
    xs_spec, xp_spec = _two_source_specs(tm)
    widths = [(512, BF16), (128, F32), (128, F32), (256, BF16), (256, BF16), (512, BF16), (512, BF16), (128, F32),
              (B_HEAD_DIM, BF16), (B_HEAD_DIM, BF16), (B_KV_HEADS * V_SLOT, BF16)]
    return pl.pallas_call(
        _in_even_kernel,
        grid=(N_GROUPS, GROUP_TOKENS // tm),
        in_specs=[xs_spec, xp_spec, _const_spec((1, D_MODEL)), _mod_spec(), _mod_spec(),
                  _const_spec(w_main.shape), _const_spec(wg_hi.shape), _const_spec(wg_lo.shape),
                  _const_spec(gb.shape), _const_spec(bd.shape), _const_spec(qnw.shape), _const_spec(knw.shape),
                  _rope_spec(tm), _rope_spec(tm), _const_spec(w_vaug.shape), _const_spec(v_one.shape)],
        out_specs=[_grp_spec(tm, w) for w, _ in widths],
        out_shape=[jax.ShapeDtypeStruct((N_GROUPS, GROUP_TOKENS, w), dt) for w, dt in widths],
        compiler_params=_cparams(2),
        name="in_even",
    )(xs, xp, nw, sc, sh, w_main, wg_hi, wg_lo, gb, bd, qnw, knw, cos, sin, w_vaug, v_one)


ATT_CHUNK = 256


def _kv_source(k_ref, k_idx, v_ref, v_idx, n, first=0):
    return n, (lambda rows: k_ref[k_idx(rows)]), (lambda rows: v_ref[v_idx(rows)]), first


def _softmax_pv_heads(heads, dv):
    def chunks_of(sources):
        out = []
        for n, load_k, load_v, first in sources:
            step = min(n, ATT_CHUNK)
            out += [(load_k, load_v, pl.ds(first + c, step)) for c in range(0, n, step)]
        return out

    plans = [(q, chunks_of(sources)) for q, sources in heads]
    scores = [[] for _ in plans]
    tile_max = [None] * len(plans)
    accs = [None] * len(plans)

    def score_chunk(h, i):
        q, chunks = plans[h]
        if i >= len(chunks):
            return
        s = _dot_nt(q, chunks[i][0](chunks[i][2]))
        scores[h].append(s)
        for j in range(0, s.shape[1], LANES):
            t = s[:, j:j + LANES]
            tile_max[h] = t if tile_max[h] is None else jnp.maximum(tile_max[h], t)

    def value_chunk(h, i, m):
        _, chunks = plans[h]
        if i >= len(chunks):
            return
        pv = _dot(jnp.exp2(scores[h][i] - m).astype(BF16), chunks[i][1](chunks[i][2]))
        accs[h] = pv if accs[h] is None else accs[h] + pv

    n_chunks = max(len(c) for _, c in plans)
    for i in range(n_chunks):
        score_chunk(0, i)
    for h in range(len(plans)):
        m = jnp.max(tile_max[h], axis=-1, keepdims=True)
        for i in range(n_chunks):
            value_chunk(h, i, m)
            if h + 1 < len(plans):
                score_chunk(h + 1, i)
    return [acc[:, :dv] / acc[:, dv:dv + 1] for acc in accs]


def _attn_branches(g, emit):
    @pl.when(g < N_SAMPLE_GROUPS)
    def _():
        emit(True)

    @pl.when(g >= N_SAMPLE_GROUPS)
    def _():
        emit(False)


def _gqa_kernel(q_ref, k0_ref, k1_ref, v_ref, kc_ref, vc_ref, o_ref):
    own = pl.multiple_of(pl.program_id(1) * SEG, SEG)
    q = q_ref[0]
    pair = 2 * B_HEAD_DIM

    def sources_of(kv, k_ref, latent):
        vl = slice(kv * V_SLOT, (kv + 1) * V_SLOT)
        new = functools.partial(_kv_source, k_ref, lambda r: (0, r), v_ref, lambda r: (0, r, vl))
        if not latent:
            return [new(SEG, own)]
        return [_kv_source(kc_ref, lambda r: (0, kv, r), vc_ref, lambda r: (0, r, vl), PAST_LEN),
                new(GROUP_TOKENS)]

    def emit(latent):
        heads = []
        for kv, k_ref in enumerate((k0_ref, k1_ref)):
            sources = sources_of(kv, k_ref, latent)
            for half in range(B_HEADS // B_KV_HEADS // 2):
                lo = (kv * B_HEADS // B_KV_HEADS + 2 * half) * B_HEAD_DIM
                heads.append((jnp.concatenate([q[:, lo:lo + B_HEAD_DIM], q[:, lo + B_HEAD_DIM:lo + pair]], axis=0),
                              sources))
        outs = []
        for o in _softmax_pv_heads(heads, B_HEAD_DIM):
            outs += [o[:SEG], o[SEG:]]
        o_ref[0] = jnp.concatenate(outs, axis=-1).astype(BF16)

    _attn_branches(pl.program_id(0), emit)


def _gqa_attention(q, k0, k1, v_aug, kc, vc):
    cache = lambda nd: (lambda g, c: (jnp.minimum(g, N_SAMPLE_GROUPS - 1),) + (0,) * (nd - 1))
    whole = lambda w: pl.BlockSpec((1, GROUP_TOKENS, w), lambda g, c: (g, 0, 0))
    return pl.pallas_call(
        _gqa_kernel,
        grid=(N_GROUPS, SEGS_PER_GROUP),
        in_specs=[pl.BlockSpec((1, SEG, B_HEADS * B_HEAD_DIM), lambda g, c: (g, c, 0)),
                  whole(B_HEAD_DIM), whole(B_HEAD_DIM), whole(B_KV_HEADS * V_SLOT),
                  pl.BlockSpec((1, B_KV_HEADS, PAST_LEN, B_HEAD_DIM), cache(4)),
                  pl.BlockSpec((1, PAST_LEN, B_KV_HEADS * V_SLOT), cache(3))],
        out_specs=pl.BlockSpec((1, SEG, B_HEADS * B_HEAD_DIM), lambda g, c: (g, c, 0)),
        out_shape=jax.ShapeDtypeStruct((N_GROUPS, GROUP_TOKENS, B_HEADS * B_HEAD_DIM), BF16),
        compiler_params=_cparams(2),
        name="gqa_attention",
    )(q, k0, k1, v_aug, kc, vc)


MLA_PAIR = 4
MLA_ROWS = 2 * SEG


def _mla_kernel(q_ref, k_ref, v_ref, kc_ref, vc_ref, o_ref):
    tile = pl.multiple_of(pl.program_id(2) * MLA_ROWS, MLA_ROWS)
    q = q_ref[0]

    def sources_of(j, latent, seg):
        kl = slice(j * C_QPAD, (j + 1) * C_QPAD)
        vl = slice(j * V_SLOT, (j + 1) * V_SLOT)
        new = functools.partial(_kv_source, k_ref, lambda r: (0, r, kl), v_ref, lambda r: (0, r, vl))
        if not latent:
            return [new(SEG, tile + seg * SEG)]
        return [_kv_source(kc_ref, lambda r: (0, r, kl), vc_ref, lambda r: (0, r, vl), PAST_LEN),
                new(GROUP_TOKENS)]

    def emit(latent):
        cols = [slice(j * C_QPAD, (j + 1) * C_QPAD) for j in range(MLA_PAIR)]
        if latent:
            outs = _softmax_pv_heads([(q[:, cols[j]], sources_of(j, True, 0)) for j in range(MLA_PAIR)], C_V)
        else:
            segs = range(MLA_ROWS // SEG)
            parts = _softmax_pv_heads([(q[s * SEG:(s + 1) * SEG, cols[j]], sources_of(j, False, s))
                                       for j in range(MLA_PAIR) for s in segs], C_V)
            outs = [jnp.concatenate(parts[j * len(segs):(j + 1) * len(segs)], axis=0) for j in range(MLA_PAIR)]
        o_ref[0] = jnp.concatenate(outs, axis=-1).astype(BF16)

    _attn_branches(pl.program_id(0), emit)


def _mla_attention(q, k, v_aug, kc, vc):
    wq, wv = MLA_PAIR * C_QPAD, MLA_PAIR * V_SLOT
    cache = lambda g, p, c: (jnp.minimum(g, N_SAMPLE_GROUPS - 1), 0, p)
    return pl.pallas_call(
        _mla_kernel,
        grid=(N_GROUPS, C_HEADS // MLA_PAIR, GROUP_TOKENS // MLA_ROWS),
        in_specs=[pl.BlockSpec((1, MLA_ROWS, wq), lambda g, p, c: (g, c, p)),
                  pl.BlockSpec((1, GROUP_TOKENS, wq), lambda g, p, c: (g, 0, p)),
                  pl.BlockSpec((1, GROUP_TOKENS, wv), lambda g, p, c: (g, 0, p)),
                  pl.BlockSpec((1, PAST_LEN, wq), cache),
                  pl.BlockSpec((1, PAST_LEN, wv), cache)],
        out_specs=pl.BlockSpec((1, MLA_ROWS, MLA_PAIR * C_V), lambda g, p, c: (g, c, p)),
        out_shape=jax.ShapeDtypeStruct((N_GROUPS, GROUP_TOKENS, C_HEADS * C_V), BF16),
        compiler_params=_cparams(3),
        name="mla_attention",
    )(q, k, v_aug, kc, vc)


def _out_kernel(*refs, even):
    if even:
        (xs_ref, xp_ref, att_ref, h_ref, gate_ref, g1_ref, w_ref, mnw_ref, n2w_ref, sc_ref, sh_ref, rwh_ref,
         rwl_ref, x1_ref, xpk_ref, aff_ref) = refs
        x = jnp.where(pl.program_id(0) >= N_SAMPLE_GROUPS, xp_ref[0], xs_ref[0])
        h = h_ref[0]
        mnw = mnw_ref[...]
        heads = [_rms(h[:, j * A_V_DIM:(j + 1) * A_V_DIM], mnw[:, j * A_V_DIM:(j + 1) * A_V_DIM])
                 for j in range(A_HEADS)]
        mix = jnp.concatenate(heads, axis=-1) * _sigmoid(gate_ref[0].astype(F32))
    else:
        (x_ref, att_ref, h_ref, gate_ref, g1_ref, w_ref, n2w_ref, sc_ref, sh_ref, rwh_ref, rwl_ref,
         x1_ref, xpk_ref, aff_ref) = refs
        x = x_ref[0]
        mix = h_ref[0] * jax.nn.gelu(gate_ref[0].astype(F32))
    half = w_ref.shape[0] // 2
    out = _dot(att_ref[0], w_ref[0:half, :]) + _dot(mix.astype(BF16), w_ref[half:, :])
    x1 = x + g1_ref[0] * out
    x1_ref[0] = x1
    xm = _rms(x1, n2w_ref[...]) * (1.0 + sc_ref[0]) + sh_ref[0]
    xpk_ref[0] = _row_tiles(xm)
    logits = _dot_x3(xm, rwh_ref[...], rwl_ref[...])
    lane = lax.broadcasted_iota(jnp.int32, logits.shape, 1)
    logits = jnp.where(lane < N_EXPERTS, logits, -jnp.inf)
    e = jnp.exp(logits - jnp.max(logits, axis=-1, keepdims=True))
    aff_ref[0] = e / jnp.sum(e, axis=-1, keepdims=True)


def _out_proj(x_args, att, h, gate, g1, w, mnw, n2w, sc, sh, rw_hi, rw_lo, *, even):
    tm = TOKEN_TILE
    if even:
        x_specs = list(_two_source_specs(tm))
        extra, extra_specs = [mnw], [_const_spec(mnw.shape)]
    else:
        x_specs = [_grp_spec(tm, D_MODEL)]
        extra, extra_specs = [], []
    return pl.pallas_call(
        functools.partial(_out_kernel, even=even),
        grid=(N_GROUPS, GROUP_TOKENS // tm),
        in_specs=x_specs + [_grp_spec(tm, 512), _grp_spec(tm, 512), _grp_spec(tm, 512), _mod_spec(),
                            _const_spec(w.shape)] + extra_specs +
                 [_const_spec(n2w.shape), _mod_spec(), _mod_spec(), _const_spec(rw_hi.shape),
                  _const_spec(rw_lo.shape)],
        out_specs=[_grp_spec(tm, D_MODEL), _row_tile_spec(tm), _grp_spec(tm, LANES)],
        out_shape=[jax.ShapeDtypeStruct((N_GROUPS, GROUP_TOKENS, D_MODEL), F32),
                   jax.ShapeDtypeStruct((N_GROUPS, GROUP_TOKENS) + ROW_TILE, F32),
                   jax.ShapeDtypeStruct((N_GROUPS, GROUP_TOKENS, LANES), F32)],
        compiler_params=_cparams(2),
        name="out_even" if even else "out_odd",
    )(*x_args, att, h, gate, g1, w, *extra, n2w, sc, sh, rw_hi, rw_lo)


def _scan_flags():
    g = pl.program_id(0)
    c = pl.program_id(1)
    cb = SEGS_PER_GROUP - 1 - c
    per_seq = jnp.where(g >= N_SAMPLE_GROUPS, 1, SEGS_PER_GROUP)
    starts = ((c % per_seq) == 0, (cb % per_seq) == per_seq - 1)
    ends = ((c % per_seq) == per_seq - 1, (cb % per_seq) == 0)
    return c, cb, starts, ends


def _store_or_add(ref, chunk, value, first_touch):
    rows = pl.ds(pl.multiple_of(chunk * SEG, SEG), SEG)

    @pl.when(first_touch)
    def _():
        ref[0, rows, :] = value

    @pl.when(jnp.logical_not(first_touch))
    def _():
        ref[0, rows, :] += value


def _log_sigmoid(x):
    return jnp.minimum(x, 0.0) - jnp.log1p(jnp.exp(-jnp.abs(x)))


C_AUG = 2 * A_V_DIM


def _mlstm_kernel(qf_ref, kf_ref, vf_ref, gf_ref, qb_ref, kb_ref, vb_ref, gb_ref, s0_ref, m0_ref,
                  h_ref, sf_ref, sb_ref, mf_ref, mb_ref, c_scr, m_scr):
    c, cb, starts, _ = _scan_flags()
    L = SEG
    row = lax.broadcasted_iota(jnp.int32, (L, L), 0)
    col = lax.broadcasted_iota(jnp.int32, (L, L), 1)
    lane = lax.broadcasted_iota(jnp.int32, (1, LANES), 1)
    ones_col = (lax.broadcasted_iota(jnp.int32, (L, A_V_DIM), 1) == 0).astype(BF16)

    for d in range(2):
        @pl.when(starts[d])
        def _():
            c_scr[d] = s0_ref[0, d]
            m_scr[d] = m0_ref[0, d]

    dirs = []
    for d, (q_ref, k_ref, v_ref, g_ref) in enumerate(((qf_ref, kf_ref, vf_ref, gf_ref),
                                                       (qb_ref, kb_ref, vb_ref, gb_ref))):
        mask = (col <= row) if d == 0 else (col >= row)
        tri = mask.astype(BF16)
        gates = g_ref[0]
        lf = _log_sigmoid(gates)
        hi, mid, lo = _split3(lf)
        cum = _dot(tri, hi) + (_dot(tri, mid) + _dot(tri, lo))
        dirs.append(dict(mask=mask, gates=gates, cum=cum, total=jnp.sum(lf, axis=0, keepdims=True),
                         q=q_ref[0], k=k_ref[0], v=v_ref[0], m_vec=m_scr[d]))
    for dd in dirs:
        dd["gates_t"] = dd["gates"].T
        dd["cum_t"] = dd["cum"].T
        dd["k_t"] = dd["k"].astype(F32).T

    chains = [dict(d=d, h=h) for d in range(2) for h in range(A_HEADS)]
    for ch in chains:
        d, h, dd = ch["d"], ch["h"], dirs[ch["d"]]
        ci, cf = d * 2 * A_HEADS + h, d * 2 * A_HEADS + A_HEADS + h
        ch["bc"], ch["br"] = dd["cum"][:, cf:cf + 1], dd["cum_t"][cf:cf + 1, :]
        ch["ir"] = dd["gates_t"][ci:ci + 1, :]
        ch["tot"] = dd["total"][:, cf:cf + 1]
        ch["m_prev"] = dd["m_vec"][:, h:h + 1]
        ch["qh"] = dd["q"][:, h * A_QK_DIM:(h + 1) * A_QK_DIM]
        ch["kh"] = dd["k"][:, h * A_QK_DIM:(h + 1) * A_QK_DIM]
        ch["v_aug"] = jnp.concatenate([dd["v"][:, h * A_V_DIM:(h + 1) * A_V_DIM], ones_col], axis=-1)
        ch["state"] = c_scr[d, h]
    for ch in chains:
        ch["qk"] = _dot_nt(ch["qh"], ch["kh"])
        ch["qc"] = _dot(ch["qh"], ch["state"].astype(BF16))
    for ch in chains:
        ch["dmat"] = jnp.where(dirs[ch["d"]]["mask"], ch["bc"] - ch["br"] + ch["ir"], -jnp.inf)
        ch["m_inter"] = ch["bc"] + ch["m_prev"]
        ch["g_row"] = ch["tot"] - ch["br"] + ch["ir"]
    for ch in chains:
        ch["m_t"] = jnp.maximum(ch["m_inter"], jnp.max(ch["dmat"], axis=-1, keepdims=True))
        ch["m_new"] = jnp.maximum(ch["tot"] + ch["m_prev"], jnp.max(ch["g_row"], axis=-1, keepdims=True))
    for ch in chains:
        ch["s"] = ch["qk"] * jnp.exp(ch["dmat"] - ch["m_t"])
        ch["w_inter"] = jnp.exp(ch["m_inter"] - ch["m_t"])
        ch["kw"] = (dirs[ch["d"]]["k_t"][ch["h"] * A_QK_DIM:(ch["h"] + 1) * A_QK_DIM, :]
                    * jnp.exp(ch["g_row"] - ch["m_new"])).astype(BF16)
    for ch in chains:
        ch["sv"] = _dot(ch["s"].astype(BF16), ch["v_aug"])
        ch["dstate"] = _dot(ch["kw"], ch["v_aug"])
    for ch in chains:
        num = ch["w_inter"] * ch["qc"][:, :A_V_DIM] + ch["sv"][:, :A_V_DIM]
        den = ch["w_inter"] * ch["qc"][:, A_V_DIM:A_V_DIM + 1] + jnp.sum(ch["s"], axis=-1, keepdims=True)
        ch["out"] = num / jnp.maximum(jnp.abs(den), jnp.exp(-ch["m_t"]))
        decay = jnp.exp(ch["tot"] + ch["m_prev"] - ch["m_new"])
        c_scr[ch["d"], ch["h"]] = decay * ch["state"] + ch["dstate"]
    for d in range(2):
        m_out = dirs[d]["m_vec"]
        for ch in chains[d * A_HEADS:(d + 1) * A_HEADS]:
            m_out = jnp.where(lane == ch["h"], ch["m_new"], m_out)
        m_scr[d] = m_out
    h_f = jnp.concatenate([ch["out"] for ch in chains[:A_HEADS]], axis=-1)
    h_b = jnp.concatenate([ch["out"] for ch in chains[A_HEADS:]], axis=-1)
    _store_or_add(h_ref, c, h_f, c < cb)
    _store_or_add(h_ref, cb, h_b, c < cb)
    sf_ref[0, 0] = c_scr[0]
    sb_ref[0, 0] = c_scr[1]
    mf_ref[0, 0] = m_scr[0]
    mb_ref[0, 0] = m_scr[1]


def _mlstm(qa, ka, va, gates, s0, m0):
    fwd = lambda w: pl.BlockSpec((1, SEG, w), lambda g, c: (g, c, 0))
    bwd = lambda w: pl.BlockSpec((1, SEG, w), lambda g, c: (g, SEGS_PER_GROUP - 1 - c, 0))
    st_shape = (1, 1, A_HEADS, A_QK_DIM, C_AUG)
    st_f = pl.BlockSpec(st_shape, lambda g, c: (g, c, 0, 0, 0))
    st_b = pl.BlockSpec(st_shape, lambda g, c: (g, SEGS_PER_GROUP - 1 - c, 0, 0, 0))
    m_f = pl.BlockSpec((1, 1, 1, LANES), lambda g, c: (g, c, 0, 0))
    m_b = pl.BlockSpec((1, 1, 1, LANES), lambda g, c: (g, SEGS_PER_GROUP - 1 - c, 0, 0))
    widths = (A_HEADS * A_QK_DIM, A_HEADS * A_QK_DIM, A_HEADS * A_V_DIM, LANES)
    st_out = jax.ShapeDtypeStruct((N_GROUPS, SEGS_PER_GROUP, A_HEADS, A_QK_DIM, C_AUG), F32)
    m_out = jax.ShapeDtypeStruct((N_GROUPS, SEGS_PER_GROUP, 1, LANES), F32)
    return pl.pallas_call(
        _mlstm_kernel,
        grid=(N_GROUPS, SEGS_PER_GROUP),
        in_specs=[fwd(w) for w in widths] + [bwd(w) for w in widths] + [
            pl.BlockSpec((1, 2, A_HEADS, A_QK_DIM, C_AUG), lambda g, c: (g, 0, 0, 0, 0)),
            pl.BlockSpec((1, 2, 1, LANES), lambda g, c: (g, 0, 0, 0))],
        out_specs=[pl.BlockSpec((1, GROUP_TOKENS, A_HEADS * A_V_DIM), lambda g, c: (g, 0, 0)),
                   st_f, st_b, m_f, m_b],
        out_shape=[jax.ShapeDtypeStruct((N_GROUPS, GROUP_TOKENS, A_HEADS * A_V_DIM), F32),
                   st_out, st_out, m_out, m_out],
        scratch_shapes=[pltpu.VMEM((2, A_HEADS, A_QK_DIM, C_AUG), F32), pltpu.VMEM((2, 1, LANES), F32)],
        compiler_params=_cparams(2),
        name="mlstm",
    )(qa, ka, va, gates, qa, ka, va, gates, s0, m0)


def _one_minus_square_of_exp(log_a):
    a = jnp.exp(log_a)
    return a, jnp.tanh(-log_a) * (1.0 + a * a)


def _linear_scan(a, b, h_in, reverse):
    n = a.shape[0] // SUBLANES
    a3 = a.reshape(n, SUBLANES, D_RNN)
    b3 = b.reshape(n, SUBLANES, D_RNN)
    row = lax.broadcasted_iota(jnp.int32, a3.shape, 1)
    for d in (1, 2, 4):
        shift = SUBLANES - d if reverse else d
        valid = (row < SUBLANES - d) if reverse else (row >= d)
        b3 = jnp.where(valid, a3 * pltpu.roll(b3, shift, 1) + b3, b3)
        a3 = jnp.where(valid, a3 * pltpu.roll(a3, shift, 1), a3)
    h = h_in
    outs = [None] * n
    for j in (reversed(range(n)) if reverse else range(n)):
        hj = a3[j] * h + b3[j]
        outs[j] = hj
        h = hj[0:1] if reverse else hj[SUBLANES - 1:SUBLANES]
    return jnp.concatenate(outs, axis=0), h


def _rglru_kernel(xf_ref, pf_ref, nf_ref, xb_ref, pb_ref, nb_ref, cw_ref, cb_ref, wa_ref, wx_ref, ba_ref,
                  bx_ref, lam_ref, h0_ref, o_ref, sf_ref, sb_ref, h_scr):
    c, cb, starts, ends = _scan_flags()
    cw = cw_ref[...]

    def run(d, x_ref, p_ref, n_ref, chunk, start, end, first_touch):
        @pl.when(start if d == 0 else end)
        def _():
            h_scr[d] = h0_ref[0, d]

        prev = jnp.where(start, 0.0, p_ref[0])
        nxt = jnp.where(end, 0.0, n_ref[0])
        xcat = jnp.concatenate([prev, x_ref[0], nxt], axis=0)
        xc = cb_ref[...]
        for j in range(CONV_W):
            off = SUBLANES - CONV_LEFT + j
            xc = xc + cw[j:j + 1, :] * xcat[off:off + SEG, :]
        xcb = xc.astype(BF16)
        r = _sigmoid(_dot(xcb, wa_ref[d]) + ba_ref[d])
        i = _sigmoid(_dot(xcb, wx_ref[d]) + bx_ref[d])
        lam = lam_ref[d]
        softplus_neg = jnp.maximum(-lam, 0.0) + jnp.log1p(jnp.exp(-jnp.abs(lam)))
        log_a = (-RG_C * r) * softplus_neg
        a, one_minus_a2 = _one_minus_square_of_exp(log_a)
        b = jnp.sqrt(one_minus_a2) * (i * xc)
        hs, h_last = _linear_scan(a, b, h_scr[d], reverse=(d == 1))
        h_scr[d] = h_last
        _store_or_add(o_ref, chunk, hs, first_touch)

    run(0, xf_ref, pf_ref, nf_ref, c, starts[0], ends[0], c < cb)
    run(1, xb_ref, pb_ref, nb_ref, cb, ends[1], starts[1], c < cb)
    sf_ref[0, 0] = h_scr[0]
    sb_ref[0, 0] = h_scr[1]


def _rglru(xr, conv_w, conv_b, wa, wx, ba, bx, lam, h0):
    blocks = SEG // SUBLANES
    n_blocks = GROUP_TOKENS // SUBLANES
    cbk = lambda c: SEGS_PER_GROUP - 1 - c
    x_f = pl.BlockSpec((1, SEG, D_RNN), lambda g, c: (g, c, 0))
    x_b = pl.BlockSpec((1, SEG, D_RNN), lambda g, c: (g, cbk(c), 0))
    halo = lambda f: pl.BlockSpec((1, SUBLANES, D_RNN), f)
    p_f = halo(lambda g, c: (g, jnp.maximum(c * blocks - 1, 0), 0))
    n_f = halo(lambda g, c: (g, jnp.minimum((c + 1) * blocks, n_blocks - 1), 0))
    p_b = halo(lambda g, c: (g, jnp.maximum(cbk(c) * blocks - 1, 0), 0))
    n_b = halo(lambda g, c: (g, jnp.minimum((cbk(c) + 1) * blocks, n_blocks - 1), 0))
    st = jax.ShapeDtypeStruct((N_GROUPS, SEGS_PER_GROUP, 1, D_RNN), F32)
    return pl.pallas_call(
        _rglru_kernel,
        grid=(N_GROUPS, SEGS_PER_GROUP),
        in_specs=[x_f, p_f, n_f, x_b, p_b, n_b, _const_spec(conv_w.shape), _const_spec(conv_b.shape),
                  _const_spec(wa.shape), _const_spec(wx.shape), _const_spec(ba.shape), _const_spec(bx.shape),
                  _const_spec(lam.shape), pl.BlockSpec((1, 2, 1, D_RNN), lambda g, c: (g, 0, 0, 0))],
        out_specs=[pl.BlockSpec((1, GROUP_TOKENS, D_RNN), lambda g, c: (g, 0, 0)),
                   pl.BlockSpec((1, 1, 1, D_RNN), lambda g, c: (g, c, 0, 0)),
                   pl.BlockSpec((1, 1, 1, D_RNN), lambda g, c: (g, cbk(c), 0, 0))],
        out_shape=[jax.ShapeDtypeStruct((N_GROUPS, GROUP_TOKENS, D_RNN), F32), st, st],
        scratch_shapes=[pltpu.VMEM((2, 1, D_RNN), F32)],
        compiler_params=_cparams(2),
        name="rglru",
    )(xr, xr, xr, xr, xr, xr, conv_w, conv_b, wa, wx, ba, bx, lam, h0)


OD_COLS = dict(cq=(0, 384), ckv=(384, 640), xr=(640, 1152), xg=(1152, 1664))
MLA_SCALE = (C_NOPE + C_ROPE) ** -0.5 * LOG2E


def _mla_keys(ckv_b, kr_b, wuk_ref, place_ref, wuv_ref):
    kcat = _dot(ckv_b, wuk_ref[...]) + _dot(kr_b, place_ref[...])
    v = _dot(ckv_b, wuv_ref[...])
    lane = lax.broadcasted_iota(jnp.int32, (1, v.shape[-1]), 1)
    return kcat.astype(BF16), (v + (lane % V_SLOT == C_V).astype(F32)).astype(BF16)


def _in_odd_kernel(x_ref, moe_ref, g2_ref, nw_ref, sc_ref, sh_ref, w_ref, wkr_ref, qnw_ref, kvnw_ref, wuq_ref,
                   wuk_ref, place_ref, wuv_ref, cq_ref, sq_ref, ck_ref, sk_ref,
                   x2_ref, q_ref, kcat_ref, v_ref, ckv_ref, kr_ref, xr_ref, xg_ref):
    x = x_ref[0] + g2_ref[0] * moe_ref[0].reshape(x_ref.shape[1:])
    x2_ref[0] = x
    xm = _rms(x, nw_ref[...]) * (1.0 + sc_ref[0]) + sh_ref[0]
    xb = xm.astype(BF16)

    def proj(name):
        lo, hi = OD_COLS[name]
        return _dot(xb, w_ref[:, lo:hi])

    cq = _rms(proj("cq"), qnw_ref[...])
    q = _dot(cq.astype(BF16), wuq_ref[...])
    q = _rope(q, _tile_lanes(cq_ref[0], C_HEADS), _tile_lanes(sq_ref[0], C_HEADS), C_ROPE // 4)
    q_ref[0] = (q * MLA_SCALE).astype(BF16)
    ckv = _rms(proj("ckv"), kvnw_ref[...])
    ckv_ref[0] = ckv
    kr = _rope(_dot(xb, wkr_ref[...]), ck_ref[0], sk_ref[0], C_ROPE // 4)
    kr_ref[0] = kr
    kcat_ref[0], v_ref[0] = _mla_keys(ckv.astype(BF16), kr.astype(BF16), wuk_ref, place_ref, wuv_ref)
    xr_ref[0] = proj("xr")
    xg_ref[0] = proj("xg").astype(BF16)


def _in_odd(x, moe, g2, nw, sc, sh, w_main, w_kr, qnw, kvnw, w_uq, w_uk, place, w_uv, cos_q, sin_q, cos_k, sin_k):
    tm = TOKEN_TILE
    widths = [(D_MODEL, F32), (C_HEADS * C_QPAD, BF16), (C_HEADS * C_QPAD, BF16), (C_HEADS * V_SLOT, BF16),
              (C_KV_LORA, F32), (LANES, F32), (D_RNN, F32), (D_RNN, BF16)]
    consts = [nw, None, None, w_main, w_kr, qnw, kvnw, w_uq, w_uk, place, w_uv]
    const_specs = [_mod_spec() if a is None else _const_spec(a.shape) for a in consts]
    return pl.pallas_call(
        _in_odd_kernel,
        grid=(N_GROUPS, GROUP_TOKENS // tm),
        in_specs=[_grp_spec(tm, D_MODEL), _row_tile_spec(tm), _mod_spec()] + const_specs +
                 [_rope_spec(tm)] * 4,
        out_specs=[_grp_spec(tm, w) for w, _ in widths],
        out_shape=[jax.ShapeDtypeStruct((N_GROUPS, GROUP_TOKENS, w), dt) for w, dt in widths],
        compiler_params=_cparams(2),
        name="in_odd",
    )(x, moe, g2, nw, sc, sh, w_main, w_kr, qnw, kvnw, w_uq, w_uk, place, w_uv, cos_q, sin_q, cos_k, sin_k)


def _kv_expand_kernel(ckv_ref, kr_ref, wuk_ref, place_ref, wuv_ref, kcat_ref, v_ref):
    kcat_ref[0], v_ref[0] = _mla_keys(ckv_ref[0].astype(BF16), kr_ref[0].astype(BF16), wuk_ref, place_ref, wuv_ref)


def _kv_expand(ckv, kr_pad, w_uk, place, w_uv):
    n = ckv.shape[0]
    cst = lambda a: pl.BlockSpec(a.shape, lambda b: (0,) * a.ndim)
    return pl.pallas_call(
        _kv_expand_kernel,
        grid=(n,),
        in_specs=[pl.BlockSpec((1, PAST_LEN, C_KV_LORA), lambda b: (b, 0, 0)),
                  pl.BlockSpec((1, PAST_LEN, LANES), lambda b: (b, 0, 0)), cst(w_uk), cst(place), cst(w_uv)],
        out_specs=[pl.BlockSpec((1, PAST_LEN, C_HEADS * C_QPAD), lambda b: (b, 0, 0)),
                   pl.BlockSpec((1, PAST_LEN, C_HEADS * V_SLOT), lambda b: (b, 0, 0))],
        out_shape=[jax.ShapeDtypeStruct((n, PAST_LEN, C_HEADS * C_QPAD), BF16),
                   jax.ShapeDtypeStruct((n, PAST_LEN, C_HEADS * V_SLOT), BF16)],
        compiler_params=_cparams(1),
        name="kv_expand",
    )(ckv, kr_pad, w_uk, place, w_uv)


def _route_kernel(aff_ref, idx_ref, cum_scr):
    is_ctx = pl.program_id(0) >= N_SAMPLE_GROUPS
    cap = jnp.where(is_ctx, CTX_CAP, SLOTS).astype(F32)
    aff = aff_ref[0].reshape(SEGS_PER_GROUP, SEG, LANES)

    def per_set(per_seg, combine):
        whole = jnp.broadcast_to(combine(per_seg, axis=0, keepdims=True), per_seg.shape)
        return jnp.where(is_ctx, per_seg, whole)

    def count(mask):
        return per_set(jnp.sum(mask.astype(F32), axis=1), jnp.sum)

    def as_float(word):
        return lax.bitcast_convert_type(word, F32)[:, None, :]

    def search(i, word):
        cand = word | lax.shift_left(jnp.int32(1), 30 - i)
        return jnp.where(count(aff >= as_float(cand)) >= cap, cand, word)

    word = lax.fori_loop(0, 31, search, jnp.zeros((SEGS_PER_GROUP, LANES), jnp.int32))
    upper = as_float(word + 1)
    kth = as_float(word)
    left = cap - count(aff >= upper)
    found = jnp.zeros(left.shape, jnp.bool_)
    for _ in range(3):
        cand = per_set(jnp.max(jnp.where(aff < upper, aff, -1.0), axis=1), jnp.max)[:, None, :]
        n_cand = count(aff == cand)
        hit = jnp.logical_and(jnp.logical_not(found), left <= n_cand)
        kth = jnp.where(hit[:, None, :], cand, kth)
        found = jnp.logical_or(found, hit)
        left = jnp.where(found, left, left - n_cand)
        upper = jnp.where(found[:, None, :], upper, cand)
    above = aff > kth
    tied = aff == kth
    need = cap - count(above)

    tri = (lax.broadcasted_iota(jnp.int32, (SEG, SEG), 1) <= lax.broadcasted_iota(jnp.int32, (SEG, SEG), 0)).astype(BF16)

    def prefix(mask, across_segments):
        m = mask.astype(BF16)
        outs = []
        offset = jnp.zeros((1, LANES), F32)
        for s in range(SEGS_PER_GROUP):
            p = _dot(tri, m[s])
            outs.append(p + jnp.where(across_segments, offset, 0.0))
            offset = offset + p[SEG - 1:SEG, :]
        return jnp.stack(outs)

    tied_rank = prefix(tied, jnp.logical_not(is_ctx)) - tied.astype(F32)
    keep = above | (tied & (tied_rank < need[:, None, :]))
    cum = prefix(keep, True)
    for s in range(SEGS_PER_GROUP):
        cum_scr[s] = cum[s].T

    part = SLOTS // 2
    lane = lax.broadcasted_iota(jnp.int32, (part, LANES), 1)
    parts = []
    for p in range(SLOTS // part):
        slot = (lax.broadcasted_iota(jnp.int32, (part, LANES), 0) + p * part).astype(F32)
        columns = jnp.zeros((part, LANES), F32)
        for e in range(N_EXPERTS):
            def block(s, acc):
                for half in range(SEG // LANES):
                    acc = acc + (cum_scr[s, e:e + 1, half * LANES:(half + 1) * LANES] <= slot).astype(F32)
                return acc
            counts = lax.fori_loop(0, SEGS_PER_GROUP, block, jnp.zeros((part, LANES), F32))
            columns = jnp.where(lane == e, jnp.sum(counts, axis=-1, keepdims=True), columns)
        parts.append(columns)
    columns = jnp.concatenate(parts, axis=0)
    rows = jnp.minimum(columns, GROUP_TOKENS - 1.0).T
    idx_ref[0] = rows[:N_EXPERTS].astype(jnp.int32)


def _route(aff):
    return pl.pallas_call(
        _route_kernel,
        grid=(N_GROUPS,),
        in_specs=[pl.BlockSpec((1, GROUP_TOKENS, LANES), lambda g: (g, 0, 0))],
        out_specs=pl.BlockSpec((1, N_EXPERTS, SLOTS), lambda g: (g, 0, 0)),
        out_shape=jax.ShapeDtypeStruct((N_GROUPS, N_EXPERTS, SLOTS), jnp.int32),
        scratch_shapes=[pltpu.VMEM((SEGS_PER_GROUP, LANES, SEG), F32)],
        compiler_params=_cparams(1),
        name="route",
    )(aff)


SCATTER_BATCH = 8


def _moe_kernel(idx_ref, xpk_ref, aff_ref, wg_ref, wu_ref, wd_ref, o_ref, xe_scr, ga_scr, ye_scr):
    g = pl.program_id(0)
    e = pl.program_id(1)
    base = (g * N_EXPERTS + e) * SLOTS

    @pl.when(e == 0)
    def _():
        o_ref[...] = jnp.zeros_like(o_ref)

    def gather(j, carry):
        s0 = pl.multiple_of(j * SCATTER_BATCH, SCATTER_BATCH)
        for k in range(SCATTER_BATCH):
            t = idx_ref[base + s0 + k]
            xe_scr[s0 + k] = xpk_ref[0, t]
            ga_scr[pl.ds(s0 + k, 1), :] = aff_ref[0, pl.ds(t, 1), :]
        return carry

    lax.fori_loop(0, SLOTS // SCATTER_BATCH, gather, 0)
    xe = xe_scr[...].reshape(SLOTS, D_MODEL).astype(BF16)
    lane = lax.broadcasted_iota(jnp.int32, (SLOTS, LANES), 1)
    gate = jnp.sum(jnp.where(lane == e, ga_scr[...], 0.0), axis=-1, keepdims=True)
    a = _dot(xe, wg_ref[0])
    hid = (a * _sigmoid(a)) * _dot(xe, wu_ref[0])
    ye_scr[...] = _row_tiles(_dot(hid.astype(BF16), wd_ref[0]) * gate)

    def scatter(j, carry):
        s0 = pl.multiple_of(j * SCATTER_BATCH, SCATTER_BATCH)
        toks = [idx_ref[base + s0 + k] for k in range(SCATTER_BATCH)]
        tiles = [o_ref[0, t] + ye_scr[s0 + k] for k, t in enumerate(toks)]
        for t, tile in zip(toks, tiles):
            o_ref[0, t] = tile
        return carry

    lax.fori_loop(0, SLOTS // SCATTER_BATCH, scatter, 0)


def _moe(idx, xpk, aff, w_gate, w_up, w_down):
    once = pl.Buffered(1)
    return pl.pallas_call(
        _moe_kernel,
        grid_spec=pltpu.PrefetchScalarGridSpec(
            num_scalar_prefetch=1,
            grid=(N_GROUPS, N_EXPERTS),
            in_specs=[
                pl.BlockSpec((1, GROUP_TOKENS) + ROW_TILE, lambda g, e, idx: (g, 0, 0, 0), pipeline_mode=once),
                pl.BlockSpec((1, GROUP_TOKENS, LANES), lambda g, e, idx: (g, 0, 0), pipeline_mode=once),
                pl.BlockSpec((1, D_MODEL, EXPERT_FF), lambda g, e, idx: (e, 0, 0)),
                pl.BlockSpec((1, D_MODEL, EXPERT_FF), lambda g, e, idx: (e, 0, 0)),
                pl.BlockSpec((1, EXPERT_FF, D_MODEL), lambda g, e, idx: (e, 0, 0)),
            ],
            out_specs=pl.BlockSpec((1, GROUP_TOKENS) + ROW_TILE, lambda g, e, idx: (g, 0, 0, 0),
                                   pipeline_mode=once),
            scratch_shapes=[pltpu.VMEM((SLOTS,) + ROW_TILE, F32), pltpu.VMEM((SLOTS, LANES), F32),
                            pltpu.VMEM((SLOTS,) + ROW_TILE, F32)],
        ),
        out_shape=jax.ShapeDtypeStruct((N_GROUPS, GROUP_TOKENS) + ROW_TILE, F32),
        compiler_params=_cparams(2),
        name="moe",
    )(idx.reshape(-1), xpk, aff, w_gate, w_up, w_down)


def _final_kernel(x_ref, moe_ref, g2_ref, nw_ref, o_ref):
    o_ref[0] = _rms(x_ref[0] + g2_ref[0] * moe_ref[0].reshape(x_ref.shape[1:]), nw_ref[...])


def _final(x, moe, g2, nw, first_group, n_groups):
    tm = TOKEN_TILE
    src = lambda w: pl.BlockSpec((1, tm, w), lambda g, i: (g + first_group, i, 0))
    return pl.pallas_call(
        _final_kernel,
        grid=(n_groups, GROUP_TOKENS // tm),
        in_specs=[src(D_MODEL), _row_tile_spec(tm, first_group),
                  pl.BlockSpec((1, 1, D_MODEL), lambda g, i: (g + first_group, 0, 0)),
                  _const_spec(nw.shape)],
        out_specs=_grp_spec(tm, D_MODEL),
        out_shape=jax.ShapeDtypeStruct((n_groups, GROUP_TOKENS, D_MODEL), F32),
        compiler_params=_cparams(2),
        name="final_norm",
    )(x, moe, g2, nw)


def _hi_lo(w):
    hi = w.astype(BF16)
    return hi, (w - hi.astype(F32)).astype(BF16)


def _pad_cols(w, width=LANES):
    return jnp.pad(w, ((0, 0),) * (w.ndim - 1) + ((0, width - w.shape[-1]),))


def _ctx_only(a):
    tail = a[N_SAMPLE_GROUPS:]
    return tail.reshape((N_CTX_GROUPS * SEGS_PER_GROUP,) + tail.shape[2:])


def _with_ctx_zeros(a):
    return jnp.concatenate([a, jnp.zeros((N_CTX_GROUPS,) + a.shape[1:], a.dtype)], axis=0)


def _moe_block(xpk, aff, w_gate, w_up, w_down):
    idx = _route(aff)
    return _moe(idx, xpk, aff, w_gate.astype(BF16), w_up.astype(BF16), w_down.astype(BF16))


def kernel(x_prompt, x_sample, state_mlstm_C, state_mlstm_n, state_mlstm_m, cache_gqa_k, cache_gqa_v, cache_mla_ckv, cache_mla_krope, state_rglru_h, c, c_ctx, norm1_w, norm2_w, final_norm_w, ada_w, ada_b, ev_w_in, ev_w_out, gqa_q_norm, gqa_k_norm, mlstm_gate_b, mlstm_norm_w, od_w_in, od_w_out, mla_q_norm, mla_kv_norm, mla_w_uq, mla_w_ukv, rg_conv_w, rg_conv_b, rg_wa, rg_ba, rg_wx, rg_bx, rg_lambda, router_w, exp_w_gate, exp_w_up, exp_w_down):
    n_ctx = x_prompt.shape[0]
    xs = x_sample
    xp = x_prompt.reshape(N_CTX_GROUPS, GROUP_TOKENS, D_MODEL)

    cond = jnp.concatenate([c, jnp.broadcast_to(c_ctx[None, :], (16 - N_SAMPLE_GROUPS, D_MODEL))], axis=0)
    mods = _adaln(cond, ada_w, ada_b)[:, :N_GROUPS]
    mod = [[m[:, None, :] for m in jnp.split(mods[layer], 6, axis=-1)] for layer in range(2)]
    router = [_hi_lo(_pad_cols(router_w[layer])) for layer in range(2)]

    sh1, sc1, g1, sh2, sc2, g2 = mod[0]
    w_in = ev_w_in[0]
    wg_hi, wg_lo = _hi_lo(_pad_cols(w_in[:, EV_MAIN:]))
    bd = jnp.asarray(np.kron(np.eye(B_HEADS), np.full((B_HEAD_DIM, B_HEAD_DIM), 1.0 / B_HEAD_DIM)), BF16)
    cos_b, sin_b = _rope_tables(B_HEAD_DIM)
    cos_b, sin_b = _pad_tables(np.tile(cos_b, (1, 2)), np.tile(sin_b, (1, 2)))
    v_lo, v_hi = EV_COLS["v"]
    w_vaug = jnp.pad(w_in[:, v_lo:v_hi].reshape(D_MODEL, B_KV_HEADS, B_HEAD_DIM),
                     ((0, 0), (0, 0), (0, V_SLOT - B_HEAD_DIM))).reshape(D_MODEL, -1).astype(BF16)
    slot_one = np.zeros((1, B_KV_HEADS, V_SLOT), np.float32)
    slot_one[:, :, B_HEAD_DIM] = 1.0
    q, kb, vb, qa, ka, va, oa, ga, k0, k1, v_aug = _in_even(
        xs, xp, norm1_w[0][None, :], sc1, sh1, w_in[:, :EV_MAIN].astype(BF16), wg_hi, wg_lo,
        _pad_cols(mlstm_gate_b[0][None, :]), bd, jnp.tile(gqa_q_norm[0], B_HEADS)[None, :],
        jnp.tile(gqa_k_norm[0], B_KV_HEADS)[None, :], cos_b, sin_b, w_vaug,
        jnp.asarray(slot_one.reshape(1, -1)))
    kc = jnp.swapaxes(cache_gqa_k[:, 0], 1, 2).astype(BF16)
    vc = jnp.pad(cache_gqa_v[:, 0], ((0, 0), (0, 0), (0, 0), (0, V_SLOT - B_HEAD_DIM))) + jnp.asarray(slot_one)
    att = _gqa_attention(q, k0, k1, v_aug, kc, vc.reshape(N_SAMPLE_GROUPS, PAST_LEN, -1).astype(BF16))
    s0 = jnp.concatenate([state_mlstm_C[:, 0], state_mlstm_n[:, 0][..., None],
                          jnp.zeros(state_mlstm_C[:, 0].shape[:-1] + (C_AUG - A_V_DIM - 1,), F32)], axis=-1)
    m0 = _pad_cols(state_mlstm_m[:, 0])[:, :, None, :]
    h_a, st_f, st_b, m_f, m_b = _mlstm(qa, ka, va, ga, _with_ctx_zeros(s0), _with_ctx_zeros(m0))
    x1, xpk, aff = _out_proj((xs, xp), att, h_a, oa, g1, ev_w_out[0].astype(BF16), mlstm_norm_w[0][None, :],
                             norm2_w[0][None, :], sc2, sh2, *router[0], even=True)
    moe0 = _moe_block(xpk, aff, exp_w_gate[0], exp_w_up[0], exp_w_down[0])
    g2_0 = g2

    sh1, sc1, g1, sh2, sc2, g2 = mod[1]
    w_in = od_w_in[0]
    o_kr = C_Q_LORA + C_KV_LORA
    w_main = jnp.concatenate([w_in[:, :o_kr], w_in[:, o_kr + C_ROPE:]], axis=1).astype(BF16)
    w_kr = _pad_cols(w_in[:, o_kr:o_kr + C_ROPE]).astype(BF16)
    uq = mla_w_uq[0].reshape(C_Q_LORA, C_HEADS, C_NOPE + C_ROPE)
    w_uq = jnp.pad(uq, ((0, 0), (0, 0), (0, C_QPAD - C_NOPE - C_ROPE))).reshape(C_Q_LORA, -1).astype(BF16)
    ukv = mla_w_ukv[0].reshape(C_KV_LORA, C_HEADS, C_NOPE + C_V)
    w_uk = jnp.pad(ukv[..., :C_NOPE], ((0, 0), (0, 0), (0, C_QPAD - C_NOPE))).reshape(C_KV_LORA, -1).astype(BF16)
    w_uv = jnp.pad(ukv[..., C_NOPE:], ((0, 0), (0, 0), (0, V_SLOT - C_V))).reshape(C_KV_LORA, -1).astype(BF16)
    place_np = np.zeros((LANES, C_HEADS, C_QPAD), np.float32)
    for j in range(C_ROPE):
        place_np[j, :, C_NOPE + j] = 1.0
    place = jnp.asarray(place_np.reshape(LANES, -1), BF16)
    cos_c, sin_c = _rope_tables(C_ROPE)
    cos_k, sin_k = _pad_tables(cos_c, sin_c)
    cos_q = np.ones((GROUP_TOKENS, C_NOPE + C_ROPE), np.float32)
    sin_q = np.zeros((GROUP_TOKENS, C_NOPE + C_ROPE), np.float32)
    cos_q[:, C_NOPE:] = cos_c
    sin_q[:, C_NOPE:] = sin_c
    cos_q, sin_q = _pad_tables(cos_q, sin_q)
    x2, q, kcat, v, ckv, kr, xr, xg = _in_odd(
        x1, moe0, g2_0, norm1_w[1][None, :], sc1, sh1, w_main, w_kr, mla_q_norm[0][None, :],
        mla_kv_norm[0][None, :], w_uq, w_uk, place, w_uv, cos_q, sin_q, cos_k, sin_k)
    kcat_c, v_c = _kv_expand(cache_mla_ckv[:, 0], _pad_cols(cache_mla_krope[:, 0].reshape(-1, C_ROPE)).reshape(
        N_SAMPLE_GROUPS, PAST_LEN, LANES), w_uk, place, w_uv)
    att = _mla_attention(q, kcat, v, kcat_c, v_c)
    eye = jnp.eye(RG_BLOCKS, dtype=F32)
    dense = lambda w: jnp.einsum("knde,nm->kndme", w, eye).reshape(2, D_RNN, D_RNN).astype(BF16)
    h0 = _with_ctx_zeros(state_rglru_h[:, 0][:, :, None, :])
    rg, rs_f, rs_b = _rglru(xr, rg_conv_w[0], rg_conv_b[0][None, :], dense(rg_wa[0]), dense(rg_wx[0]),
                            rg_ba[0][:, None, :], rg_bx[0][:, None, :], rg_lambda[0][:, None, :], h0)
    x3, xpk, aff = _out_proj((x2,), att, rg, xg, g1, od_w_out[0].astype(BF16), None, norm2_w[1][None, :],
                             sc2, sh2, *router[1], even=False)
    moe1 = _moe_block(xpk, aff, exp_w_gate[1], exp_w_up[1], exp_w_down[1])

    fnw = final_norm_w[None, :]
    y_sample = _final(x3, moe1, g2, fnw, 0, N_SAMPLE_GROUPS)
    y_prompt = _final(x3, moe1, g2, fnw, N_SAMPLE_GROUPS, N_CTX_GROUPS).reshape(n_ctx, SEG, D_MODEL)

    st = jnp.stack([_ctx_only(st_f), _ctx_only(st_b)], axis=1)
    new_c = st[..., :A_V_DIM][:, None]
    new_n = st[..., A_V_DIM][:, None]
    new_m = jnp.stack([_ctx_only(m_f)[:, 0, :A_HEADS], _ctx_only(m_b)[:, 0, :A_HEADS]], axis=1)[:, None]
    new_gk = kb[N_SAMPLE_GROUPS:].reshape(n_ctx, 1, SEG, B_KV_HEADS, B_HEAD_DIM)
    new_gv = vb[N_SAMPLE_GROUPS:].reshape(n_ctx, 1, SEG, B_KV_HEADS, B_HEAD_DIM)
    new_ckv = ckv[N_SAMPLE_GROUPS:].reshape(n_ctx, 1, SEG, C_KV_LORA)
    new_kr = kr[N_SAMPLE_GROUPS:, :, :C_ROPE].reshape(n_ctx, 1, SEG, C_ROPE)
    new_rh = jnp.stack([_ctx_only(rs_f)[:, 0], _ctx_only(rs_b)[:, 0]], axis=1)[:, None]
    return (y_prompt, y_sample, new_c, new_n, new_m, new_gk, new_gv, new_ckv, new_kr, new_rh)
```

```python
import functools

import numpy as np
import jax
import jax.numpy as jnp
from jax import lax
from jax.experimental import pallas as pl
from jax.experimental.pallas import tpu as pltpu

F32 = jnp.float32
BF16 = jnp.bfloat16

D_MODEL = 1024
N_SAMPLE_GROUPS = 8
N_CTX_GROUPS = 2
N_GROUPS = N_SAMPLE_GROUPS + N_CTX_GROUPS
GROUP_TOKENS = 4096
SEG = 256
SEGS_PER_GROUP = GROUP_TOKENS // SEG
PAST_LEN = 512
GRID_W = 64
EPS = 1e-6
ROPE_BASE = 10000.0

B_HEADS, B_KV_HEADS, B_HEAD_DIM = 8, 2, 64
A_HEADS, A_QK_DIM, A_V_DIM = 4, 64, 128
C_HEADS, C_Q_LORA, C_KV_LORA, C_NOPE, C_ROPE, C_V = 8, 384, 256, 64, 32, 64
C_QPAD = 128
D_RNN, RG_BLOCKS, RG_C, CONV_W, CONV_LEFT = 512, 8, 8.0, 4, 2
N_EXPERTS, EXPERT_FF = 16, 512
SLOTS = 512
CTX_CAP = 2 * SEG // N_EXPERTS
LANES = 128
SUBLANES = 8
TOKEN_TILE = 512
VMEM_LIMIT = 56 * 1024 * 1024


def _cparams(n_axes, vmem=VMEM_LIMIT):
    return pltpu.CompilerParams(dimension_semantics=("arbitrary",) * n_axes, vmem_limit_bytes=vmem)


def _dot(a, b):
    return jnp.dot(a, b, preferred_element_type=F32)


def _dot_nt(a, b):
    return lax.dot_general(a, b, (((1,), (1,)), ((), ())), preferred_element_type=F32)


def _split3(a):
    hi = a.astype(BF16)
    r = a - hi.astype(F32)
    mid = r.astype(BF16)
    lo = (r - mid.astype(F32)).astype(BF16)
    return hi, mid, lo


def _dot_x3(a, w_hi, w_lo):
    a_hi = a.astype(BF16)
    a_lo = (a - a_hi.astype(F32)).astype(BF16)
    return _dot(a_hi, w_hi) + (_dot(a_lo, w_hi) + _dot(a_hi, w_lo))


def _rms(x, w):
    return (x * lax.rsqrt(jnp.mean(x * x, axis=-1, keepdims=True) + EPS)) * w


def _sigmoid(x):
    return 1.0 / (1.0 + jnp.exp(-x))


def _rope(x, cos, sin_signed, half):
    n = x.shape[-1]
    lane = lax.broadcasted_iota(jnp.int32, x.shape, x.ndim - 1)
    first = (lane % (2 * half)) < half
    partner = jnp.where(first, pltpu.roll(x, n - half, x.ndim - 1), pltpu.roll(x, half, x.ndim - 1))
    return x * cos + partner * sin_signed


def _tile_lanes(a, reps):
    return a if reps == 1 else jnp.concatenate([a] * reps, axis=-1)


def _adaln_kernel(c_ref, w_ref, b_ref, o_ref):
    c = c_ref[...]
    a = c * _sigmoid(c)
    w = w_ref[0]
    w_hi = w.astype(BF16)
    w_lo = (w - w_hi.astype(F32)).astype(BF16)
    o_ref[0] = _dot_x3(a, w_hi, w_lo) + b_ref[0]


def _adaln(cond, ada_w, ada_b):
    depth, _, n6 = ada_w.shape
    rows = cond.shape[0]
    tn = 1024
    return pl.pallas_call(
        _adaln_kernel,
        grid=(depth, n6 // tn),
        in_specs=[
            pl.BlockSpec((rows, D_MODEL), lambda l, j: (0, 0)),
            pl.BlockSpec((1, D_MODEL, tn), lambda l, j: (l, 0, j)),
            pl.BlockSpec((1, 1, tn), lambda l, j: (l, 0, j)),
        ],
        out_specs=pl.BlockSpec((1, rows, tn), lambda l, j: (l, 0, j)),
        out_shape=jax.ShapeDtypeStruct((depth, rows, n6), F32),
        compiler_params=_cparams(2),
        name="adaln",
    )(cond, ada_w, ada_b.reshape(depth, 1, n6))


def _grp_spec(tm, width):
    return pl.BlockSpec((1, tm, width), lambda g, i: (g, i, 0))


ROW_TILE = (D_MODEL // LANES, LANES)


def _row_tiles(x):
    return x.reshape((x.shape[0],) + ROW_TILE)


def _row_tile_spec(tm, first_group=0):
    return pl.BlockSpec((1, tm) + ROW_TILE, lambda g, i: (g + first_group, i, 0, 0))


def _mod_spec():
    return pl.BlockSpec((1, 1, D_MODEL), lambda g, i: (g, 0, 0))


def _const_spec(shape):
    nd = len(shape)
    return pl.BlockSpec(shape, lambda g, i: (0,) * nd)


def _two_source_specs(tm):
    last = GROUP_TOKENS // tm - 1
    xs = pl.BlockSpec((1, tm, D_MODEL), lambda g, i: (jnp.minimum(g, N_SAMPLE_GROUPS - 1),
                                                      jnp.where(g < N_SAMPLE_GROUPS, i, last), 0))
    xp = pl.BlockSpec((1, tm, D_MODEL), lambda g, i: (jnp.maximum(g - N_SAMPLE_GROUPS, 0),
                                                      jnp.where(g < N_SAMPLE_GROUPS, 0, i), 0))
    return xs, xp


def _rope_spec(tm):
    return pl.BlockSpec((1, tm, LANES), lambda g, i: (jnp.where(g < N_SAMPLE_GROUPS, 0, 1), i, 0))


def _rope_tables(d):
    half = d // 4
    t = np.arange(GROUP_TOKENS)
    pos = np.stack([(t // GRID_W).astype(np.float32), (t % GRID_W).astype(np.float32)], axis=1)
    lane = np.arange(d)
    axis = lane // (d // 2)
    j = lane % half
    first = (lane % (d // 2)) < half
    inv_freq = (np.float32(ROPE_BASE) ** (-(np.arange(half, dtype=np.float32)) / np.float32(half))).astype(np.float32)
    ang = (pos[:, axis] * inv_freq[j][None, :]).astype(np.float32)
    cos = np.cos(ang.astype(np.float64)).astype(np.float32)
    sin = np.sin(ang.astype(np.float64)).astype(np.float32)
    sin = np.where(first[None, :], -sin, sin)
    return cos, sin


def _pad_tables(cos, sin):
    t, w = cos.shape
    cos_p = np.ones((t, LANES), np.float32)
    sin_p = np.zeros((t, LANES), np.float32)
    cos_p[:, :w] = cos
    sin_p[:, :w] = sin
    cos2 = np.stack([cos_p, np.ones_like(cos_p)])
    sin2 = np.stack([sin_p, np.zeros_like(sin_p)])
    return jnp.asarray(cos2), jnp.asarray(sin2)


EV_COLS = dict(q=(0, 512), k=(512, 640), v=(640, 768), qa=(768, 1024), ka=(1024, 1280), va=(1280, 1792),
               oa=(1792, 2304))
EV_MAIN = 2304


LOG2E = 1.4426950408889634
V_SLOT = LANES


def _in_even_kernel(xs_ref, xp_ref, nw_ref, sc_ref, sh_ref, w_ref, wgh_ref, wgl_ref, gb_ref, bd_ref, qnw_ref,
                    knw_ref, cos_ref, sin_ref, wv_ref, vone_ref,
                    q_ref, kb_ref, vb_ref, qa_ref, ka_ref, va_ref, oa_ref, ga_ref, k0_ref, k1_ref, vaug_ref):
    g = pl.program_id(0)
    x = jnp.where(g >= N_SAMPLE_GROUPS, xp_ref[0], xs_ref[0])
    xm = _rms(x, nw_ref[...]) * (1.0 + sc_ref[0]) + sh_ref[0]
    xb = xm.astype(BF16)
    cos = cos_ref[0]
    sin = sin_ref[0]

    def proj(name):
        lo, hi = EV_COLS[name]
        return _dot(xb, w_ref[:, lo:hi])

    q = proj("q")
    q = q * lax.rsqrt(_dot((q * q).astype(BF16), bd_ref[...]) + EPS) * qnw_ref[...]
    q = _rope(q, _tile_lanes(cos, 4), _tile_lanes(sin, 4), B_HEAD_DIM // 4)
    q_ref[0] = (q * (B_HEAD_DIM ** -0.5 * LOG2E)).astype(BF16)
    k = proj("k")
    k = k * lax.rsqrt(_dot((k * k).astype(BF16), bd_ref[0:LANES, 0:LANES]) + EPS) * knw_ref[...]
    k = _rope(k, cos, sin, B_HEAD_DIM // 4)
    kb_ref[0] = k
    k0_ref[0] = k[:, :B_HEAD_DIM].astype(BF16)
    k1_ref[0] = k[:, B_HEAD_DIM:].astype(BF16)
    vb_ref[0] = proj("v")
    vaug_ref[0] = (_dot(xb, wv_ref[...]) + vone_ref[...]).astype(BF16)
    qa_ref[0] = proj("qa").astype(BF16)
    ka_ref[0] = (proj("ka") * (A_QK_DIM ** -0.5)).astype(BF16)
    va_ref[0] = proj("va").astype(BF16)
    oa_ref[0] = proj("oa").astype(BF16)
    ga_ref[0] = _dot_x3(xm, wgh_ref[...], wgl_ref[...]) + gb_ref[...]


def _in_even(xs, xp, nw, sc, sh, w_main, wg_hi, wg_lo, gb, bd, qnw, knw, cos, sin, w_vaug, v_one):
    tm = TOKEN_TILE
    xs_spec, xp_spec = _two_source_specs(tm)
    widths = [(512, BF16), (128, F32), (128, F32), (256, BF16), (256, BF16), (512, BF16), (512, BF16), (128, F32),
              (B_HEAD_DIM, BF16), (B_HEAD_DIM, BF16), (B_KV_HEADS * V_SLOT, BF16)]
    return pl.pallas_call(
        _in_even_kernel,
        grid=(N_GROUPS, GROUP_TOKENS // tm),
        in_specs=[xs_spec, xp_spec, _const_spec((1, D_MODEL)), _mod_spec(), _mod_spec(),
                  _const_spec(w_main.shape), _const_spec(wg_hi.shape), _const_spec(wg_lo.shape),
                  _const_spec(gb.shape), _const_spec(bd.shape), _const_spec(qnw.shape), _const_spec(knw.shape),
                  _rope_spec(tm), _rope_spec(tm), _const_spec(w_vaug.shape), _const_spec(v_one.shape)],
        out_specs=[_grp_spec(tm, w) for w, _ in widths],
        out_shape=[jax.ShapeDtypeStruct((N_GROUPS, GROUP_TOKENS, w), dt) for w, dt in widths],
        compiler_params=_cparams(2),
        name="in_even",
    )(xs, xp, nw, sc, sh, w_main, wg_hi, wg_lo, gb, bd, qnw, knw, cos, sin, w_vaug, v_one)


ATT_CHUNK = 256


def _kv_source(k_ref, k_idx, v_ref, v_idx, n, first=0):
    return n, (lambda rows: k_ref[k_idx(rows)]), (lambda rows: v_ref[v_idx(rows)]), first


def _softmax_pv_heads(heads, dv):
    def chunks_of(sources):
        out = []
        for n, load_k, load_v, first in sources:
            step = min(n, ATT_CHUNK)
            out += [(load_k, load_v, pl.ds(first + c, step)) for c in range(0, n, step)]
        return out

    plans = [(q, chunks_of(sources)) for q, sources in heads]
    scores = [[] for _ in plans]
    tile_max = [None] * len(plans)
    accs = [None] * len(plans)

    def score_chunk(h, i):
        q, chunks = plans[h]
        if i >= len(chunks):
            return
        s = _dot_nt(q, chunks[i][0](chunks[i][2]))
        scores[h].append(s)
        for j in range(0, s.shape[1], LANES):
            t = s[:, j:j + LANES]
            tile_max[h] = t if tile_max[h] is None else jnp.maximum(tile_max[h], t)

    def value_chunk(h, i, m):
        _, chunks = plans[h]
        if i >= len(chunks):
            return
        pv = _dot(jnp.exp2(scores[h][i] - m).astype(BF16), chunks[i][1](chunks[i][2]))
        accs[h] = pv if accs[h] is None else accs[h] + pv

    n_chunks = max(len(c) for _, c in plans)
    for i in range(n_chunks):
        score_chunk(0, i)
    for h in range(len(plans)):
        m = jnp.max(tile_max[h], axis=-1, keepdims=True)
        for i in range(n_chunks):
            value_chunk(h, i, m)
            if h + 1 < len(plans):
                score_chunk(h + 1, i)
    return [acc[:, :dv] / acc[:, dv:dv + 1] for acc in accs]


def _attn_branches(g, emit):
    @pl.when(g < N_SAMPLE_GROUPS)
    def _():
        emit(True)

    @pl.when(g >= N_SAMPLE_GROUPS)
    def _():
        emit(False)


def _gqa_kernel(q_ref, k0_ref, k1_ref, v_ref, kc_ref, vc_ref, o_ref):
    own = pl.multiple_of(pl.program_id(1) * SEG, SEG)
    q = q_ref[0]
    pair = 2 * B_HEAD_DIM

    def sources_of(kv, k_ref, latent):
        vl = slice(kv * V_SLOT, (kv + 1) * V_SLOT)
        new = functools.partial(_kv_source, k_ref, lambda r: (0, r), v_ref, lambda r: (0, r, vl))
        if not latent:
            return [new(SEG, own)]
        return [_kv_source(kc_ref, lambda r: (0, kv, r), vc_ref, lambda r: (0, r, vl), PAST_LEN),
                new(GROUP_TOKENS)]

    def emit(latent):
        heads = []
        for kv, k_ref in enumerate((k0_ref, k1_ref)):
            sources = sources_of(kv, k_ref, latent)
            for half in range(B_HEADS // B_KV_HEADS // 2):
                lo = (kv * B_HEADS // B_KV_HEADS + 2 * half) * B_HEAD_DIM
                heads.append((jnp.concatenate([q[:, lo:lo + B_HEAD_DIM], q[:, lo + B_HEAD_DIM:lo + pair]], axis=0),
                              sources))
        outs = []
        for o in _softmax_pv_heads(heads, B_HEAD_DIM):
            outs += [o[:SEG], o[SEG:]]
        o_ref[0] = jnp.concatenate(outs, axis=-1).astype(BF16)

    _attn_branches(pl.program_id(0), emit)


def _gqa_attention(q, k0, k1, v_aug, kc, vc):
    cache = lambda nd: (lambda g, c: (jnp.minimum(g, N_SAMPLE_GROUPS - 1),) + (0,) * (nd - 1))
    whole = lambda w: pl.BlockSpec((1, GROUP_TOKENS, w), lambda g, c: (g, 0, 0))
    return pl.pallas_call(
        _gqa_kernel,
        grid=(N_GROUPS, SEGS_PER_GROUP),
        in_specs=[pl.BlockSpec((1, SEG, B_HEADS * B_HEAD_DIM), lambda g, c: (g, c, 0)),
                  whole(B_HEAD_DIM), whole(B_HEAD_DIM), whole(B_KV_HEADS * V_SLOT),
                  pl.BlockSpec((1, B_KV_HEADS, PAST_LEN, B_HEAD_DIM), cache(4)),
                  pl.BlockSpec((1, PAST_LEN, B_KV_HEADS * V_SLOT), cache(3))],
        out_specs=pl.BlockSpec((1, SEG, B_HEADS * B_HEAD_DIM), lambda g, c: (g, c, 0)),
        out_shape=jax.ShapeDtypeStruct((N_GROUPS, GROUP_TOKENS, B_HEADS * B_HEAD_DIM), BF16),
        compiler_params=_cparams(2),
        name="gqa_attention",
    )(q, k0, k1, v_aug, kc, vc)


MLA_PAIR = 4
MLA_ROWS = 2 * SEG


def _mla_kernel(q_ref, k_ref, v_ref, kc_ref, vc_ref, o_ref):
    tile = pl.multiple_of(pl.program_id(2) * MLA_ROWS, MLA_ROWS)
    q = q_ref[0]

    def sources_of(j, latent, seg):
        kl = slice(j * C_QPAD, (j + 1) * C_QPAD)
        vl = slice(j * V_SLOT, (j + 1) * V_SLOT)
        new = functools.partial(_kv_source, k_ref, lambda r: (0, r, kl), v_ref, lambda r: (0, r, vl))
        if not latent:
            return [new(SEG, tile + seg * SEG)]
        return [_kv_source(kc_ref, lambda r: (0, r, kl), vc_ref, lambda r: (0, r, vl), PAST_LEN),
                new(GROUP_TOKENS)]

    def emit(latent):
        cols = [slice(j * C_QPAD, (j + 1) * C_QPAD) for j in range(MLA_PAIR)]
        if latent:
            outs = _softmax_pv_heads([(q[:, cols[j]], sources_of(j, True, 0)) for j in range(MLA_PAIR)], C_V)
        else:
            segs = range(MLA_ROWS // SEG)
            parts = _softmax_pv_heads([(q[s * SEG:(s + 1) * SEG, cols[j]], sources_of(j, False, s))
                                       for j in range(MLA_PAIR) for s in segs], C_V)
            outs = [jnp.concatenate(parts[j * len(segs):(j + 1) * len(segs)], axis=0) for j in range(MLA_PAIR)]
        o_ref[0] = jnp.concatenate(outs, axis=-1).astype(BF16)

    _attn_branches(pl.program_id(0), emit)


def _mla_attention(q, k, v_aug, kc, vc):
    wq, wv = MLA_PAIR * C_QPAD, MLA_PAIR * V_SLOT
    cache = lambda g, p, c: (jnp.minimum(g, N_SAMPLE_GROUPS - 1), 0, p)
    return pl.pallas_call(
        _mla_kernel,
        grid=(N_GROUPS, C_HEADS // MLA_PAIR, GROUP_TOKENS // MLA_ROWS),
        in_specs=[pl.BlockSpec((1, MLA_ROWS, wq), lambda g, p, c: (g, c, p)),
                  pl.BlockSpec((1, GROUP_TOKENS, wq), lambda g, p, c: (g, 0, p)),
                  pl.BlockSpec((1, GROUP_TOKENS, wv), lambda g, p, c: (g, 0, p)),
                  pl.BlockSpec((1, PAST_LEN, wq), cache),
                  pl.BlockSpec((1, PAST_LEN, wv), cache)],
        out_specs=pl.BlockSpec((1, MLA_ROWS, MLA_PAIR * C_V), lambda g, p, c: (g, c, p)),
        out_shape=jax.ShapeDtypeStruct((N_GROUPS, GROUP_TOKENS, C_HEADS * C_V), BF16),
        compiler_params=_cparams(3),
        name="mla_attention",
    )(q, k, v_aug, kc, vc)


def _out_kernel(*refs, even):
    if even:
        (xs_ref, xp_ref, att_ref, h_ref, gate_ref, g1_ref, w_ref, mnw_ref, n2w_ref, sc_ref, sh_ref, rwh_ref,
         rwl_ref, x1_ref, xpk_ref, aff_ref) = refs
        x = jnp.where(pl.program_id(0) >= N_SAMPLE_GROUPS, xp_ref[0], xs_ref[0])
        h = h_ref[0]
        mnw = mnw_ref[...]
        heads = [_rms(h[:, j * A_V_DIM:(j + 1) * A_V_DIM], mnw[:, j * A_V_DIM:(j + 1) * A_V_DIM])
                 for j in range(A_HEADS)]
        mix = jnp.concatenate(heads, axis=-1) * _sigmoid(gate_ref[0].astype(F32))
    else:
        (x_ref, att_ref, h_ref, gate_ref, g1_ref, w_ref, n2w_ref, sc_ref, sh_ref, rwh_ref, rwl_ref,
         x1_ref, xpk_ref, aff_ref) = refs
        x = x_ref[0]
        mix = h_ref[0] * jax.nn.gelu(gate_ref[0].astype(F32))
    half = w_ref.shape[0] // 2
    out = _dot(att_ref[0], w_ref[0:half, :]) + _dot(mix.astype(BF16), w_ref[half:, :])
    x1 = x + g1_ref[0] * out
    x1_ref[0] = x1
    xm = _rms(x1, n2w_ref[...]) * (1.0 + sc_ref[0]) + sh_ref[0]
    xpk_ref[0] = _row_tiles(xm)
    logits = _dot_x3(xm, rwh_ref[...], rwl_ref[...])
    lane = lax.broadcasted_iota(jnp.int32, logits.shape, 1)
    logits = jnp.where(lane < N_EXPERTS, logits, -jnp.inf)
    e = jnp.exp(logits - jnp.max(logits, axis=-1, keepdims=True))
    aff_ref[0] = e / jnp.sum(e, axis=-1, keepdims=True)


def _out_proj(x_args, att, h, gate, g1, w, mnw, n2w, sc, sh, rw_hi, rw_lo, *, even):
    tm = TOKEN_TILE
    if even:
        x_specs = list(_two_source_specs(tm))
        extra, extra_specs = [mnw], [_const_spec(mnw.shape)]
    else:
        x_specs = [_grp_spec(tm, D_MODEL)]
        extra, extra_specs = [], []
    return pl.pallas_call(
        functools.partial(_out_kernel, even=even),
        grid=(N_GROUPS, GROUP_TOKENS // tm),
        in_specs=x_specs + [_grp_spec(tm, 512), _grp_spec(tm, 512), _grp_spec(tm, 512), _mod_spec(),
                            _const_spec(w.shape)] + extra_specs +
                 [_const_spec(n2w.shape), _mod_spec(), _mod_spec(), _const_spec(rw_hi.shape),
                  _const_spec(rw_lo.shape)],
        out_specs=[_grp_spec(tm, D_MODEL), _row_tile_spec(tm), _grp_spec(tm, LANES)],
        out_shape=[jax.ShapeDtypeStruct((N_GROUPS, GROUP_TOKENS, D_MODEL), F32),
                   jax.ShapeDtypeStruct((N_GROUPS, GROUP_TOKENS) + ROW_TILE, F32),
                   jax.ShapeDtypeStruct((N_GROUPS, GROUP_TOKENS, LANES), F32)],
        compiler_params=_cparams(2),
        name="out_even" if even else "out_odd",
    )(*x_args, att, h, gate, g1, w, *extra, n2w, sc, sh, rw_hi, rw_lo)


def _scan_flags():
    g = pl.program_id(0)
    c = pl.program_id(1)
    cb = SEGS_PER_GROUP - 1 - c
    per_seq = jnp.where(g >= N_SAMPLE_GROUPS, 1, SEGS_PER_GROUP)
    starts = ((c % per_seq) == 0, (cb % per_seq) == per_seq - 1)
    ends = ((c % per_seq) == per_seq - 1, (cb % per_seq) == 0)
    return c, cb, starts, ends


def _store_or_add(ref, chunk, value, first_touch):
    rows = pl.ds(pl.multiple_of(chunk * SEG, SEG), SEG)

    @pl.when(first_touch)
    def _():
        ref[0, rows, :] = value

    @pl.when(jnp.logical_not(first_touch))
    def _():
        ref[0, rows, :] += value


def _log_sigmoid(x):
    return jnp.minimum(x, 0.0) - jnp.log1p(jnp.exp(-jnp.abs(x)))


C_AUG = 2 * A_V_DIM


def _mlstm_kernel(qf_ref, kf_ref, vf_ref, gf_ref, qb_ref, kb_ref, vb_ref, gb_ref, s0_ref, m0_ref,
                  h_ref, sf_ref, sb_ref, mf_ref, mb_ref, c_scr, m_scr):
    c, cb, starts, _ = _scan_flags()
    L = SEG
    row = lax.broadcasted_iota(jnp.int32, (L, L), 0)
    col = lax.broadcasted_iota(jnp.int32, (L, L), 1)
    lane = lax.broadcasted_iota(jnp.int32, (1, LANES), 1)
    ones_col = (lax.broadcasted_iota(jnp.int32, (L, A_V_DIM), 1) == 0).astype(BF16)

    for d in range(2):
        @pl.when(starts[d])
        def _():
            c_scr[d] = s0_ref[0, d]
            m_scr[d] = m0_ref[0, d]

    dirs = []
    for d, (q_ref, k_ref, v_ref, g_ref) in enumerate(((qf_ref, kf_ref, vf_ref, gf_ref),
                                                       (qb_ref, kb_ref, vb_ref, gb_ref))):
        mask = (col <= row) if d == 0 else (col >= row)
        tri = mask.astype(BF16)
        gates = g_ref[0]
        lf = _log_sigmoid(gates)
        hi, mid, lo = _split3(lf)
        cum = _dot(tri, hi) + (_dot(tri, mid) + _dot(tri, lo))
        dirs.append(dict(mask=mask, gates=gates, cum=cum, total=jnp.sum(lf, axis=0, keepdims=True),
                         q=q_ref[0], k=k_ref[0], v=v_ref[0], m_vec=m_scr[d]))
    for dd in dirs:
        dd["gates_t"] = dd["gates"].T
        dd["cum_t"] = dd["cum"].T
        dd["k_t"] = dd["k"].astype(F32).T

    chains = [dict(d=d, h=h) for d in range(2) for h in range(A_HEADS)]
    for ch in chains:
        d, h, dd = ch["d"], ch["h"], dirs[ch["d"]]
        ci, cf = d * 2 * A_HEADS + h, d * 2 * A_HEADS + A_HEADS + h
        ch["bc"], ch["br"] = dd["cum"][:, cf:cf + 1], dd["cum_t"][cf:cf + 1, :]
        ch["ir"] = dd["gates_t"][ci:ci + 1, :]
        ch["tot"] = dd["total"][:, cf:cf + 1]
        ch["m_prev"] = dd["m_vec"][:, h:h + 1]
        ch["qh"] = dd["q"][:, h * A_QK_DIM:(h + 1) * A_QK_DIM]
        ch["kh"] = dd["k"][:, h * A_QK_DIM:(h + 1) * A_QK_DIM]
        ch["v_aug"] = jnp.concatenate([dd["v"][:, h * A_V_DIM:(h + 1) * A_V_DIM], ones_col], axis=-1)
        ch["state"] = c_scr[d, h]
    for ch in chains:
        ch["qk"] = _dot_nt(ch["qh"], ch["kh"])
        ch["qc"] = _dot(ch["qh"], ch["state"].astype(BF16))
    for ch in chains:
        ch["dmat"] = jnp.where(dirs[ch["d"]]["mask"], ch["bc"] - ch["br"] + ch["ir"], -jnp.inf)
        ch["m_inter"] = ch["bc"] + ch["m_prev"]
        ch["g_row"] = ch["tot"] - ch["br"] + ch["ir"]
    for ch in chains:
        ch["m_t"] = jnp.maximum(ch["m_inter"], jnp.max(ch["dmat"], axis=-1, keepdims=True))
        ch["m_new"] = jnp.maximum(ch["tot"] + ch["m_prev"], jnp.max(ch["g_row"], axis=-1, keepdims=True))
    for ch in chains:
        ch["s"] = ch["qk"] * jnp.exp(ch["dmat"] - ch["m_t"])
        ch["w_inter"] = jnp.exp(ch["m_inter"] - ch["m_t"])
        ch["kw"] = (dirs[ch["d"]]["k_t"][ch["h"] * A_QK_DIM:(ch["h"] + 1) * A_QK_DIM, :]
                    * jnp.exp(ch["g_row"] - ch["m_new"])).astype(BF16)
    for ch in chains:
        ch["sv"] = _dot(ch["s"].astype(BF16), ch["v_aug"])
        ch["dstate"] = _dot(ch["kw"], ch["v_aug"])
    for ch in chains:
        num = ch["w_inter"] * ch["qc"][:, :A_V_DIM] + ch["sv"][:, :A_V_DIM]
        den = ch["w_inter"] * ch["qc"][:, A_V_DIM:A_V_DIM + 1] + jnp.sum(ch["s"], axis=-1, keepdims=True)
        ch["out"] = num / jnp.maximum(jnp.abs(den), jnp.exp(-ch["m_t"]))
        decay = jnp.exp(ch["tot"] + ch["m_prev"] - ch["m_new"])
        c_scr[ch["d"], ch["h"]] = decay * ch["state"] + ch["dstate"]
    for d in range(2):
        m_out = dirs[d]["m_vec"]
        for ch in chains[d * A_HEADS:(d + 1) * A_HEADS]:
            m_out = jnp.where(lane == ch["h"], ch["m_new"], m_out)
        m_scr[d] = m_out
    h_f = jnp.concatenate([ch["out"] for ch in chains[:A_HEADS]], axis=-1)
    h_b = jnp.concatenate([ch["out"] for ch in chains[A_HEADS:]], axis=-1)
    _store_or_add(h_ref, c, h_f, c < cb)
    _store_or_add(h_ref, cb, h_b, c < cb)
    sf_ref[0, 0] = c_scr[0]
    sb_ref[0, 0] = c_scr[1]
    mf_ref[0, 0] = m_scr[0]
    mb_ref[0, 0] = m_scr[1]


def _mlstm(qa, ka, va, gates, s0, m0):
    fwd = lambda w: pl.BlockSpec((1, SEG, w), lambda g, c: (g, c, 0))
    bwd = lambda w: pl.BlockSpec((1, SEG, w), lambda g, c: (g, SEGS_PER_GROUP - 1 - c, 0))
    st_shape = (1, 1, A_HEADS, A_QK_DIM, C_AUG)
    st_f = pl.BlockSpec(st_shape, lambda g, c: (g, c, 0, 0, 0))
    st_b = pl.BlockSpec(st_shape, lambda g, c: (g, SEGS_PER_GROUP - 1 - c, 0, 0, 0))
    m_f = pl.BlockSpec((1, 1, 1, LANES), lambda g, c: (g, c, 0, 0))
    m_b = pl.BlockSpec((1, 1, 1, LANES), lambda g, c: (g, SEGS_PER_GROUP - 1 - c, 0, 0))
    widths = (A_HEADS * A_QK_DIM, A_HEADS * A_QK_DIM, A_HEADS * A_V_DIM, LANES)
    st_out = jax.ShapeDtypeStruct((N_GROUPS, SEGS_PER_GROUP, A_HEADS, A_QK_DIM, C_AUG), F32)
    m_out = jax.ShapeDtypeStruct((N_GROUPS, SEGS_PER_GROUP, 1, LANES), F32)
    return pl.pallas_call(
        _mlstm_kernel,
        grid=(N_GROUPS, SEGS_PER_GROUP),
        in_specs=[fwd(w) for w in widths] + [bwd(w) for w in widths] + [
            pl.BlockSpec((1, 2, A_HEADS, A_QK_DIM, C_AUG), lambda g, c: (g, 0, 0, 0, 0)),
            pl.BlockSpec((1, 2, 1, LANES), lambda g, c: (g, 0, 0, 0))],
        out_specs=[pl.BlockSpec((1, GROUP_TOKENS, A_HEADS * A_V_DIM), lambda g, c: (g, 0, 0)),
                   st_f, st_b, m_f, m_b],
        out_shape=[jax.ShapeDtypeStruct((N_GROUPS, GROUP_TOKENS, A_HEADS * A_V_DIM), F32),
                   st_out, st_out, m_out, m_out],
        scratch_shapes=[pltpu.VMEM((2, A_HEADS, A_QK_DIM, C_AUG), F32), pltpu.VMEM((2, 1, LANES), F32)],
        compiler_params=_cparams(2),
        name="mlstm",
    )(qa, ka, va, gates, qa, ka, va, gates, s0, m0)


def _one_minus_square_of_exp(log_a):
    a = jnp.exp(log_a)
    return a, jnp.tanh(-log_a) * (1.0 + a * a)


def _linear_scan(a, b, h_in, reverse):
    n = a.shape[0] // SUBLANES
    a3 = a.reshape(n, SUBLANES, D_RNN)
    b3 = b.reshape(n, SUBLANES, D_RNN)
    row = lax.broadcasted_iota(jnp.int32, a3.shape, 1)
    for d in (1, 2, 4):
        shift = SUBLANES - d if reverse else d
        valid = (row < SUBLANES - d) if reverse else (row >= d)
        b3 = jnp.where(valid, a3 * pltpu.roll(b3, shift, 1) + b3, b3)
        a3 = jnp.where(valid, a3 * pltpu.roll(a3, shift, 1), a3)
    h = h_in
    outs = [None] * n
    for j in (reversed(range(n)) if reverse else range(n)):
        hj = a3[j] * h + b3[j]
        outs[j] = hj
        h = hj[0:1] if reverse else hj[SUBLANES - 1:SUBLANES]
    return jnp.concatenate(outs, axis=0), h


def _rglru_kernel(xf_ref, pf_ref, nf_ref, xb_ref, pb_ref, nb_ref, cw_ref, cb_ref, wa_ref, wx_ref, ba_ref,
                  bx_ref, lam_ref, h0_ref, o_ref, sf_ref, sb_ref, h_scr):
    c, cb, starts, ends = _scan_flags()
    cw = cw_ref[...]

    def run(d, x_ref, p_ref, n_ref, chunk, start, end, first_touch):
        @pl.when(start if d == 0 else end)
        def _():
            h_scr[d] = h0_ref[0, d]

        prev = jnp.where(start, 0.0, p_ref[0])
        nxt = jnp.where(end, 0.0, n_ref[0])
        xcat = jnp.concatenate([prev, x_ref[0], nxt], axis=0)
        xc = cb_ref[...]
        for j in range(CONV_W):
            off = SUBLANES - CONV_LEFT + j
            xc = xc + cw[j:j + 1, :] * xcat[off:off + SEG, :]
        xcb = xc.astype(BF16)
        r = _sigmoid(_dot(xcb, wa_ref[d]) + ba_ref[d])
        i = _sigmoid(_dot(xcb, wx_ref[d]) + bx_ref[d])
        lam = lam_ref[d]
        softplus_neg = jnp.maximum(-lam, 0.0) + jnp.log1p(jnp.exp(-jnp.abs(lam)))
        log_a = (-RG_C * r) * softplus_neg
        a, one_minus_a2 = _one_minus_square_of_exp(log_a)
        b = jnp.sqrt(one_minus_a2) * (i * xc)
        hs, h_last = _linear_scan(a, b, h_scr[d], reverse=(d == 1))
        h_scr[d] = h_last
        _store_or_add(o_ref, chunk, hs, first_touch)

    run(0, xf_ref, pf_ref, nf_ref, c, starts[0], ends[0], c < cb)
    run(1, xb_ref, pb_ref, nb_ref, cb, ends[1], starts[1], c < cb)
    sf_ref[0, 0] = h_scr[0]
    sb_ref[0, 0] = h_scr[1]


def _rglru(xr, conv_w, conv_b, wa, wx, ba, bx, lam, h0):
    blocks = SEG // SUBLANES
    n_blocks = GROUP_TOKENS // SUBLANES
    cbk = lambda c: SEGS_PER_GROUP - 1 - c
    x_f = pl.BlockSpec((1, SEG, D_RNN), lambda g, c: (g, c, 0))
    x_b = pl.BlockSpec((1, SEG, D_RNN), lambda g, c: (g, cbk(c), 0))
    halo = lambda f: pl.BlockSpec((1, SUBLANES, D_RNN), f)
    p_f = halo(lambda g, c: (g, jnp.maximum(c * blocks - 1, 0), 0))
    n_f = halo(lambda g, c: (g, jnp.minimum((c + 1) * blocks, n_blocks - 1), 0))
    p_b = halo(lambda g, c: (g, jnp.maximum(cbk(c) * blocks - 1, 0), 0))
    n_b = halo(lambda g, c: (g, jnp.minimum((cbk(c) + 1) * blocks, n_blocks - 1), 0))
    st = jax.ShapeDtypeStruct((N_GROUPS, SEGS_PER_GROUP, 1, D_RNN), F32)
    return pl.pallas_call(
        _rglru_kernel,
        grid=(N_GROUPS, SEGS_PER_GROUP),
        in_specs=[x_f, p_f, n_f, x_b, p_b, n_b, _const_spec(conv_w.shape), _const_spec(conv_b.shape),
                  _const_spec(wa.shape), _const_spec(wx.shape), _const_spec(ba.shape), _const_spec(bx.shape),
                  _const_spec(lam.shape), pl.BlockSpec((1, 2, 1, D_RNN), lambda g, c: (g, 0, 0, 0))],
        out_specs=[pl.BlockSpec((1, GROUP_TOKENS, D_RNN), lambda g, c: (g, 0, 0)),
                   pl.BlockSpec((1, 1, 1, D_RNN), lambda g, c: (g, c, 0, 0)),
                   pl.BlockSpec((1, 1, 1, D_RNN), lambda g, c: (g, cbk(c), 0, 0))],
        out_shape=[jax.ShapeDtypeStruct((N_GROUPS, GROUP_TOKENS, D_RNN), F32), st, st],
        scratch_shapes=[pltpu.VMEM((2, 1, D_RNN), F32)],
        compiler_params=_cparams(2),
        name="rglru",
    )(xr, xr, xr, xr, xr, xr, conv_w, conv_b, wa, wx, ba, bx, lam, h0)


OD_COLS = dict(cq=(0, 384), ckv=(384, 640), xr=(640, 1152), xg=(1152, 1664))
MLA_SCALE = (C_NOPE + C_ROPE) ** -0.5 * LOG2E


def _mla_keys(ckv_b, kr_b, wuk_ref, place_ref, wuv_ref):
    kcat = _dot(ckv_b, wuk_ref[...]) + _dot(kr_b, place_ref[...])
    v = _dot(ckv_b, wuv_ref[...])
    lane = lax.broadcasted_iota(jnp.int32, (1, v.shape[-1]), 1)
    return kcat.astype(BF16), (v + (lane % V_SLOT == C_V).astype(F32)).astype(BF16)


def _in_odd_kernel(x_ref, moe_ref, g2_ref, nw_ref, sc_ref, sh_ref, w_ref, wkr_ref, qnw_ref, kvnw_ref, wuq_ref,
                   wuk_ref, place_ref, wuv_ref, cq_ref, sq_ref, ck_ref, sk_ref,
                   x2_ref, q_ref, kcat_ref, v_ref, ckv_ref, kr_ref, xr_ref, xg_ref):
    x = x_ref[0] + g2_ref[0] * moe_ref[0].reshape(x_ref.shape[1:])
    x2_ref[0] = x
    xm = _rms(x, nw_ref[...]) * (1.0 + sc_ref[0]) + sh_ref[0]
    xb = xm.astype(BF16)

    def proj(name):
        lo, hi = OD_COLS[name]
        return _dot(xb, w_ref[:, lo:hi])

    cq = _rms(proj("cq"), qnw_ref[...])
    q = _dot(cq.astype(BF16), wuq_ref[...])
    q = _rope(q, _tile_lanes(cq_ref[0], C_HEADS), _tile_lanes(sq_ref[0], C_HEADS), C_ROPE // 4)
    q_ref[0] = (q * MLA_SCALE).astype(BF16)
    ckv = _rms(proj("ckv"), kvnw_ref[...])
    ckv_ref[0] = ckv
    kr = _rope(_dot(xb, wkr_ref[...]), ck_ref[0], sk_ref[0], C_ROPE // 4)
    kr_ref[0] = kr
    kcat_ref[0], v_ref[0] = _mla_keys(ckv.astype(BF16), kr.astype(BF16), wuk_ref, place_ref, wuv_ref)
    xr_ref[0] = proj("xr")
    xg_ref[0] = proj("xg").astype(BF16)


def _in_odd(x, moe, g2, nw, sc, sh, w_main, w_kr, qnw, kvnw, w_uq, w_uk, place, w_uv, cos_q, sin_q, cos_k, sin_k):
    tm = TOKEN_TILE
    widths = [(D_MODEL, F32), (C_HEADS * C_QPAD, BF16), (C_HEADS * C_QPAD, BF16), (C_HEADS * V_SLOT, BF16),
              (C_KV_LORA, F32), (LANES, F32), (D_RNN, F32), (D_RNN, BF16)]
    consts = [nw, None, None, w_main, w_kr, qnw, kvnw, w_uq, w_uk, place, w_uv]
    const_specs = [_mod_spec() if a is None else _const_spec(a.shape) for a in consts]
    return pl.pallas_call(
        _in_odd_kernel,
        grid=(N_GROUPS, GROUP_TOKENS // tm),
        in_specs=[_grp_spec(tm, D_MODEL), _row_tile_spec(tm), _mod_spec()] + const_specs +
                 [_rope_spec(tm)] * 4,
        out_specs=[_grp_spec(tm, w) for w, _ in widths],
        out_shape=[jax.ShapeDtypeStruct((N_GROUPS, GROUP_TOKENS, w), dt) for w, dt in widths],
        compiler_params=_cparams(2),
        name="in_odd",
    )(x, moe, g2, nw, sc, sh, w_main, w_kr, qnw, kvnw, w_uq, w_uk, place, w_uv, cos_q, sin_q, cos_k, sin_k)


def _kv_expand_kernel(ckv_ref, kr_ref, wuk_ref, place_ref, wuv_ref, kcat_ref, v_ref):
    kcat_ref[0], v_ref[0] = _mla_keys(ckv_ref[0].astype(BF16), kr_ref[0].astype(BF16), wuk_ref, place_ref, wuv_ref)


def _kv_expand(ckv, kr_pad, w_uk, place, w_uv):
    n = ckv.shape[0]
    cst = lambda a: pl.BlockSpec(a.shape, lambda b: (0,) * a.ndim)
    return pl.pallas_call(
        _kv_expand_kernel,
        grid=(n,),
        in_specs=[pl.BlockSpec((1, PAST_LEN, C_KV_LORA), lambda b: (b, 0, 0)),
                  pl.BlockSpec((1, PAST_LEN, LANES), lambda b: (b, 0, 0)), cst(w_uk), cst(place), cst(w_uv)],
        out_specs=[pl.BlockSpec((1, PAST_LEN, C_HEADS * C_QPAD), lambda b: (b, 0, 0)),
                   pl.BlockSpec((1, PAST_LEN, C_HEADS * V_SLOT), lambda b: (b, 0, 0))],
        out_shape=[jax.ShapeDtypeStruct((n, PAST_LEN, C_HEADS * C_QPAD), BF16),
                   jax.ShapeDtypeStruct((n, PAST_LEN, C_HEADS * V_SLOT), BF16)],
        compiler_params=_cparams(1),
        name="kv_expand",
    )(ckv, kr_pad, w_uk, place, w_uv)


def _route_kernel(aff_ref, idx_ref, cum_scr):
    is_ctx = pl.program_id(0) >= N_SAMPLE_GROUPS
    cap = jnp.where(is_ctx, CTX_CAP, SLOTS).astype(F32)
    aff = aff_ref[0].reshape(SEGS_PER_GROUP, SEG, LANES)

    def per_set(per_seg, combine):
        whole = jnp.broadcast_to(combine(per_seg, axis=0, keepdims=True), per_seg.shape)
        return jnp.where(is_ctx, per_seg, whole)

    def count(mask):
        return per_set(jnp.sum(mask.astype(F32), axis=1), jnp.sum)

    def as_float(word):
        return lax.bitcast_convert_type(word, F32)[:, None, :]

    def search(i, word):
        cand = word | lax.shift_left(jnp.int32(1), 30 - i)
        return jnp.where(count(aff >= as_float(cand)) >= cap, cand, word)

    word = lax.fori_loop(0, 31, search, jnp.zeros((SEGS_PER_GROUP, LANES), jnp.int32))
    upper = as_float(word + 1)
    kth = as_float(word)
    left = cap - count(aff >= upper)
    found = jnp.zeros(left.shape, jnp.bool_)
    for _ in range(3):
        cand = per_set(jnp.max(jnp.where(aff < upper, aff, -1.0), axis=1), jnp.max)[:, None, :]
        n_cand = count(aff == cand)
        hit = jnp.logical_and(jnp.logical_not(found), left <= n_cand)
        kth = jnp.where(hit[:, None, :], cand, kth)
        found = jnp.logical_or(found, hit)
        left = jnp.where(found, left, left - n_cand)
        upper = jnp.where(found[:, None, :], upper, cand)
    above = aff > kth
    tied = aff == kth
    need = cap - count(above)

    tri = (lax.broadcasted_iota(jnp.int32, (SEG, SEG), 1) <= lax.broadcasted_iota(jnp.int32, (SEG, SEG), 0)).astype(BF16)

    def prefix(mask, across_segments):
        m = mask.astype(BF16)
        outs = []
        offset = jnp.zeros((1, LANES), F32)
        for s in range(SEGS_PER_GROUP):
            p = _dot(tri, m[s])
            outs.append(p + jnp.where(across_segments, offset, 0.0))
            offset = offset + p[SEG - 1:SEG, :]
        return jnp.stack(outs)

    tied_rank = prefix(tied, jnp.logical_not(is_ctx)) - tied.astype(F32)
    keep = above | (tied & (tied_rank < need[:, None, :]))
    cum = prefix(keep, True)
    for s in range(SEGS_PER_GROUP):
        cum_scr[s] = cum[s].T

    part = SLOTS // 2
    lane = lax.broadcasted_iota(jnp.int32, (part, LANES), 1)
    parts = []
    for p in range(SLOTS // part):
        slot = (lax.broadcasted_iota(jnp.int32, (part, LANES), 0) + p * part).astype(F32)
        columns = jnp.zeros((part, LANES), F32)
        for e in range(N_EXPERTS):
            def block(s, acc):
                for half in range(SEG // LANES):
                    acc = acc + (cum_scr[s, e:e + 1, half * LANES:(half + 1) * LANES] <= slot).astype(F32)
                return acc
            counts = lax.fori_loop(0, SEGS_PER_GROUP, block, jnp.zeros((part, LANES), F32))
            columns = jnp.where(lane == e, jnp.sum(counts, axis=-1, keepdims=True), columns)
        parts.append(columns)
    columns = jnp.concatenate(parts, axis=0)
    rows = jnp.minimum(columns, GROUP_TOKENS - 1.0).T
    idx_ref[0] = rows[:N_EXPERTS].astype(jnp.int32)


def _route(aff):
    return pl.pallas_call(
        _route_kernel,
        grid=(N_GROUPS,),
        in_specs=[pl.BlockSpec((1, GROUP_TOKENS, LANES), lambda g: (g, 0, 0))],
        out_specs=pl.BlockSpec((1, N_EXPERTS, SLOTS), lambda g: (g, 0, 0)),
        out_shape=jax.ShapeDtypeStruct((N_GROUPS, N_EXPERTS, SLOTS), jnp.int32),
        scratch_shapes=[pltpu.VMEM((SEGS_PER_GROUP, LANES, SEG), F32)],
        compiler_params=_cparams(1),
        name="route",
    )(aff)


SCATTER_BATCH = 8


def _moe_kernel(idx_ref, xpk_ref, aff_ref, wg_ref, wu_ref, wd_ref, o_ref, xe_a, xe_b, ga_a, ga_b, ye_a, ye_b):
    g = pl.program_id(0)
    e = pl.program_id(1)
    base = (g * N_EXPERTS + e) * SLOTS

    def gather(first, xe, ga):
        for s in range(SLOTS):
            t = idx_ref[first + s]
            xe[s] = xpk_ref[0, t]
            ga[pl.ds(s, 1), :] = aff_ref[0, pl.ds(t, 1), :]

    def scatter(first, ye):
        for s0 in range(0, SLOTS, SCATTER_BATCH):
            toks = [idx_ref[first + s0 + k] for k in range(SCATTER_BATCH)]
            tiles = [o_ref[0, t] + ye[s0 + k] for k, t in enumerate(toks)]
            for t, tile in zip(toks, tiles):
                o_ref[0, t] = tile

    @pl.when(e == 0)
    def _():
        o_ref[...] = jnp.zeros_like(o_ref)
        ye_b[...] = jnp.zeros_like(ye_b)
        gather(base, xe_a, ga_a)

    def step(xe_cur, ga_cur, ye_cur, xe_next, ga_next, ye_prev):
        gather(base + jnp.where(e < N_EXPERTS - 1, SLOTS, 0), xe_next, ga_next)
        xe = xe_cur[...].reshape(SLOTS, D_MODEL).astype(BF16)
        lane = lax.broadcasted_iota(jnp.int32, (SLOTS, LANES), 1)
        gate = jnp.sum(jnp.where(lane == e, ga_cur[...], 0.0), axis=-1, keepdims=True)
        a = _dot(xe, wg_ref[0])
        hid = (a * _sigmoid(a)) * _dot(xe, wu_ref[0])
        ye = _row_tiles(_dot(hid.astype(BF16), wd_ref[0]) * gate)
        scatter(base - jnp.where(e > 0, SLOTS, 0), ye_prev)
        ye_cur[...] = ye

    @pl.when(e % 2 == 0)
    def _():
        step(xe_a, ga_a, ye_a, xe_b, ga_b, ye_b)

    @pl.when(e % 2 == 1)
    def _():
        step(xe_b, ga_b, ye_b, xe_a, ga_a, ye_a)

    @pl.when(e == N_EXPERTS - 1)
    def _():
        scatter(base, ye_b if (N_EXPERTS - 1) % 2 else ye_a)


def _moe(idx, xpk, aff, w_gate, w_up, w_down):
    once = pl.Buffered(1)
    return pl.pallas_call(
        _moe_kernel,
        grid_spec=pltpu.PrefetchScalarGridSpec(
            num_scalar_prefetch=1,
            grid=(N_GROUPS, N_EXPERTS),
            in_specs=[
                pl.BlockSpec((1, GROUP_TOKENS) + ROW_TILE, lambda g, e, idx: (g, 0, 0, 0), pipeline_mode=once),
                pl.BlockSpec((1, GROUP_TOKENS, LANES), lambda g, e, idx: (g, 0, 0), pipeline_mode=once),
                pl.BlockSpec((1, D_MODEL, EXPERT_FF), lambda g, e, idx: (e, 0, 0)),
                pl.BlockSpec((1, D_MODEL, EXPERT_FF), lambda g, e, idx: (e, 0, 0)),
                pl.BlockSpec((1, EXPERT_FF, D_MODEL), lambda g, e, idx: (e, 0, 0)),
            ],
            out_specs=pl.BlockSpec((1, GROUP_TOKENS) + ROW_TILE, lambda g, e, idx: (g, 0, 0, 0),
                                   pipeline_mode=once),
            scratch_shapes=[pltpu.VMEM((SLOTS,) + ROW_TILE, F32), pltpu.VMEM((SLOTS,) + ROW_TILE, F32),
                            pltpu.VMEM((SLOTS, LANES), F32), pltpu.VMEM((SLOTS, LANES), F32),
                            pltpu.VMEM((SLOTS,) + ROW_TILE, F32), pltpu.VMEM((SLOTS,) + ROW_TILE, F32)],
        ),
        out_shape=jax.ShapeDtypeStruct((N_GROUPS, GROUP_TOKENS) + ROW_TILE, F32),
        compiler_params=_cparams(2),
        name="moe",
    )(idx.reshape(-1), xpk, aff, w_gate, w_up, w_down)


def _final_kernel(x_ref, moe_ref, g2_ref, nw_ref, o_ref):
    o_ref[0] = _rms(x_ref[0] + g2_ref[0] * moe_ref[0].reshape(x_ref.shape[1:]), nw_ref[...])


def _final(x, moe, g2, nw, first_group, n_groups):
    tm = TOKEN_TILE
    src = lambda w: pl.BlockSpec((1, tm, w), lambda g, i: (g + first_group, i, 0))
    return pl.pallas_call(
        _final_kernel,
        grid=(n_groups, GROUP_TOKENS // tm),
        in_specs=[src(D_MODEL), _row_tile_spec(tm, first_group),
                  pl.BlockSpec((1, 1, D_MODEL), lambda g, i: (g + first_group, 0, 0)),
                  _const_spec(nw.shape)],
        out_specs=_grp_spec(tm, D_MODEL),
        out_shape=jax.ShapeDtypeStruct((n_groups, GROUP_TOKENS, D_MODEL), F32),
        compiler_params=_cparams(2),
        name="final_norm",
    )(x, moe, g2, nw)


def _hi_lo(w):
    hi = w.astype(BF16)
    return hi, (w - hi.astype(F32)).astype(BF16)


def _pad_cols(w, width=LANES):
    return jnp.pad(w, ((0, 0),) * (w.ndim - 1) + ((0, width - w.shape[-1]),))


def _ctx_only(a):
    tail = a[N_SAMPLE_GROUPS:]
    return tail.reshape((N_CTX_GROUPS * SEGS_PER_GROUP,) + tail.shape[2:])


def _with_ctx_zeros(a):
    return jnp.concatenate([a, jnp.zeros((N_CTX_GROUPS,) + a.shape[1:], a.dtype)], axis=0)


def _moe_block(xpk, aff, w_gate, w_up, w_down):
    idx = _route(aff)
    return _moe(idx, xpk, aff, w_gate.astype(BF16), w_up.astype(BF16), w_down.astype(BF16))


def kernel(x_prompt, x_sample, state_mlstm_C, state_mlstm_n, state_mlstm_m, cache_gqa_k, cache_gqa_v, cache_mla_ckv, cache_mla_krope, state_rglru_h, c, c_ctx, norm1_w, norm2_w, final_norm_w, ada_w, ada_b, ev_w_in, ev_w_out, gqa_q_norm, gqa_k_norm, mlstm_gate_b, mlstm_norm_w, od_w_in, od_w_out, mla_q_norm, mla_kv_norm, mla_w_uq, mla_w_ukv, rg_conv_w, rg_conv_b, rg_wa, rg_ba, rg_wx, rg_bx, rg_lambda, router_w, exp_w_gate, exp_w_up, exp_w_down):
    n_ctx = x_prompt.shape[0]
    xs = x_sample
    xp = x_prompt.reshape(N_CTX_GROUPS, GROUP_TOKENS, D_MODEL)

    cond = jnp.concatenate([c, jnp.broadcast_to(c_ctx[None, :], (16 - N_SAMPLE_GROUPS, D_MODEL))], axis=0)
    mods = _adaln(cond, ada_w, ada_b)[:, :N_GROUPS]
    mod = [[m[:, None, :] for m in jnp.split(mods[layer], 6, axis=-1)] for layer in range(2)]
    router = [_hi_lo(_pad_cols(router_w[layer])) for layer in range(2)]

    sh1, sc1, g1, sh2, sc2, g2 = mod[0]
    w_in = ev_w_in[0]
    wg_hi, wg_lo = _hi_lo(_pad_cols(w_in[:, EV_MAIN:]))
    bd = jnp.asarray(np.kron(np.eye(B_HEADS), np.full((B_HEAD_DIM, B_HEAD_DIM), 1.0 / B_HEAD_DIM)), BF16)
    cos_b, sin_b = _rope_tables(B_HEAD_DIM)
    cos_b, sin_b = _pad_tables(np.tile(cos_b, (1, 2)), np.tile(sin_b, (1, 2)))
    v_lo, v_hi = EV_COLS["v"]
    w_vaug = jnp.pad(w_in[:, v_lo:v_hi].reshape(D_MODEL, B_KV_HEADS, B_HEAD_DIM),
                     ((0, 0), (0, 0), (0, V_SLOT - B_HEAD_DIM))).reshape(D_MODEL, -1).astype(BF16)
    slot_one = np.zeros((1, B_KV_HEADS, V_SLOT), np.float32)
    slot_one[:, :, B_HEAD_DIM] = 1.0
    q, kb, vb, qa, ka, va, oa, ga, k0, k1, v_aug = _in_even(
        xs, xp, norm1_w[0][None, :], sc1, sh1, w_in[:, :EV_MAIN].astype(BF16), wg_hi, wg_lo,
        _pad_cols(mlstm_gate_b[0][None, :]), bd, jnp.tile(gqa_q_norm[0], B_HEADS)[None, :],
        jnp.tile(gqa_k_norm[0], B_KV_HEADS)[None, :], cos_b, sin_b, w_vaug,
        jnp.asarray(slot_one.reshape(1, -1)))
    kc = jnp.swapaxes(cache_gqa_k[:, 0], 1, 2).astype(BF16)
    vc = jnp.pad(cache_gqa_v[:, 0], ((0, 0), (0, 0), (0, 0), (0, V_SLOT - B_HEAD_DIM))) + jnp.asarray(slot_one)
    att = _gqa_attention(q, k0, k1, v_aug, kc, vc.reshape(N_SAMPLE_GROUPS, PAST_LEN, -1).astype(BF16))
    s0 = jnp.concatenate([state_mlstm_C[:, 0], state_mlstm_n[:, 0][..., None],
                          jnp.zeros(state_mlstm_C[:, 0].shape[:-1] + (C_AUG - A_V_DIM - 1,), F32)], axis=-1)
    m0 = _pad_cols(state_mlstm_m[:, 0])[:, :, None, :]
    h_a, st_f, st_b, m_f, m_b = _mlstm(qa, ka, va, ga, _with_ctx_zeros(s0), _with_ctx_zeros(m0))
    x1, xpk, aff = _out_proj((xs, xp), att, h_a, oa, g1, ev_w_out[0].astype(BF16), mlstm_norm_w[0][None, :],
                             norm2_w[0][None, :], sc2, sh2, *router[0], even=True)
    moe0 = _moe_block(xpk, aff, exp_w_gate[0], exp_w_up[0], exp_w_down[0])
    g2_0 = g2

    sh1, sc1, g1, sh2, sc2, g2 = mod[1]
    w_in = od_w_in[0]
    o_kr = C_Q_LORA + C_KV_LORA
    w_main = jnp.concatenate([w_in[:, :o_kr], w_in[:, o_kr + C_ROPE:]], axis=1).astype(BF16)
    w_kr = _pad_cols(w_in[:, o_kr:o_kr + C_ROPE]).astype(BF16)
    uq = mla_w_uq[0].reshape(C_Q_LORA, C_HEADS, C_NOPE + C_ROPE)
    w_uq = jnp.pad(uq, ((0, 0), (0, 0), (0, C_QPAD - C_NOPE - C_ROPE))).reshape(C_Q_LORA, -1).astype(BF16)
    ukv = mla_w_ukv[0].reshape(C_KV_LORA, C_HEADS, C_NOPE + C_V)
    w_uk = jnp.pad(ukv[..., :C_NOPE], ((0, 0), (0, 0), (0, C_QPAD - C_NOPE))).reshape(C_KV_LORA, -1).astype(BF16)
    w_uv = jnp.pad(ukv[..., C_NOPE:], ((0, 0), (0, 0), (0, V_SLOT - C_V))).reshape(C_KV_LORA, -1).astype(BF16)
    place_np = np.zeros((LANES, C_HEADS, C_QPAD), np.float32)
    for j in range(C_ROPE):
        place_np[j, :, C_NOPE + j] = 1.0
    place = jnp.asarray(place_np.reshape(LANES, -1), BF16)
    cos_c, sin_c = _rope_tables(C_ROPE)
    cos_k, sin_k = _pad_tables(cos_c, sin_c)
    cos_q = np.ones((GROUP_TOKENS, C_NOPE + C_ROPE), np.float32)
    sin_q = np.zeros((GROUP_TOKENS, C_NOPE + C_ROPE), np.float32)
    cos_q[:, C_NOPE:] = cos_c
    sin_q[:, C_NOPE:] = sin_c
    cos_q, sin_q = _pad_tables(cos_q, sin_q)
    x2, q, kcat, v, ckv, kr, xr, xg = _in_odd(
        x1, moe0, g2_0, norm1_w[1][None, :], sc1, sh1, w_main, w_kr, mla_q_norm[0][None, :],
        mla_kv_norm[0][None, :], w_uq, w_uk, place, w_uv, cos_q, sin_q, cos_k, sin_k)
    kcat_c, v_c = _kv_expand(cache_mla_ckv[:, 0], _pad_cols(cache_mla_krope[:, 0].reshape(-1, C_ROPE)).reshape(
        N_SAMPLE_GROUPS, PAST_LEN, LANES), w_uk, place, w_uv)
    att = _mla_attention(q, kcat, v, kcat_c, v_c)
    eye = jnp.eye(RG_BLOCKS, dtype=F32)
    dense = lambda w: jnp.einsum("knde,nm->kndme", w, eye).reshape(2, D_RNN, D_RNN).astype(BF16)
    h0 = _with_ctx_zeros(state_rglru_h[:, 0][:, :, None, :])
    rg, rs_f, rs_b = _rglru(xr, rg_conv_w[0], rg_conv_b[0][None, :], dense(rg_wa[0]), dense(rg_wx[0]),
                            rg_ba[0][:, None, :], rg_bx[0][:, None, :], rg_lambda[0][:, None, :], h0)
    x3, xpk, aff = _out_proj((x2,), att, rg, xg, g1, od_w_out[0].astype(BF16), None, norm2_w[1][None, :],
                             sc2, sh2, *router[1], even=False)
    moe1 = _moe_block(xpk, aff, exp_w_gate[1], exp_w_up[1], exp_w_down[1])

    fnw = final_norm_w[None, :]
    y_sample = _final(x3, moe1, g2, fnw, 0, N_SAMPLE_GROUPS)
    y_prompt = _final(x3, moe1, g2, fnw, N_SAMPLE_GROUPS, N_CTX_GROUPS).reshape(n_ctx, SEG, D_MODEL)

    st = jnp.stack([_ctx_only(st_f), _ctx_only(st_b)], axis=1)
    new_c = st[..., :A_V_DIM][:, None]
    new_n = st[..., A_V_DIM][:, None]
    new_m = jnp.stack([_ctx_only(m_f)[:, 0, :A_HEADS], _ctx_only(m_b)[:, 0, :A_HEADS]], axis=1)[:, None]
    new_gk = kb[N_SAMPLE_GROUPS:].reshape(n_ctx, 1, SEG, B_KV_HEADS, B_HEAD_DIM)
    new_gv = vb[N_SAMPLE_GROUPS:].reshape(n_ctx, 1, SEG, B_KV_HEADS, B_HEAD_DIM)
    new_ckv = ckv[N_SAMPLE_GROUPS:].reshape(n_ctx, 1, SEG, C_KV_LORA)
    new_kr = kr[N_SAMPLE_GROUPS:, :, :C_ROPE].reshape(n_ctx, 1, SEG, C_ROPE)
    new_rh = jnp.stack([_ctx_only(rs_f)[:, 0], _ctx_only(rs_b)[:, 0]], axis=1)[:, None]
    return (y_prompt, y_sample, new_c, new_n, new_m, new_gk, new_gv, new_ckv, new_kr, new_rh)
```

```python
import functools

import numpy as np
import jax
import jax.numpy as jnp
from jax import lax
from jax.experimental import pallas as pl
from jax.experimental.pallas import tpu as pltpu

F32 = jnp.float32
BF16 = jnp.bfloat16

D_MODEL = 1024
N_SAMPLE_GROUPS = 8
N_CTX_GROUPS = 2
N_GROUPS = N_SAMPLE_GROUPS + N_CTX_GROUPS
GROUP_TOKENS = 4096
SEG = 256
SEGS_PER_GROUP = GROUP_TOKENS // SEG
PAST_LEN = 512
GRID_W = 64
EPS = 1e-6
ROPE_BASE = 10000.0

B_HEADS, B_KV_HEADS, B_HEAD_DIM = 8, 2, 64
A_HEADS, A_QK_DIM, A_V_DIM = 4, 64, 128
C_HEADS, C_Q_LORA, C_KV_LORA, C_NOPE, C_ROPE, C_V = 8, 384, 256, 64, 32, 64
C_QPAD = 128
D_RNN, RG_BLOCKS, RG_C, CONV_W, CONV_LEFT = 512, 8, 8.0, 4, 2
N_EXPERTS, EXPERT_FF = 16, 512
SLOTS = 512
CTX_CAP = 2 * SEG // N_EXPERTS
LANES = 128
SUBLANES = 8
TOKEN_TILE = 512
VMEM_LIMIT = 56 * 1024 * 1024


def _cparams(n_axes, vmem=VMEM_LIMIT):
    return pltpu.CompilerParams(dimension_semantics=("arbitrary",) * n_axes, vmem_limit_bytes=vmem)


def _dot(a, b):
    return jnp.dot(a, b, preferred_element_type=F32)


def _dot_nt(a, b):
    return lax.dot_general(a, b, (((1,), (1,)), ((), ())), preferred_element_type=F32)


def _split3(a):
    hi = a.astype(BF16)
    r = a - hi.astype(F32)
    mid = r.astype(BF16)
    lo = (r - mid.astype(F32)).astype(BF16)
    return hi, mid, lo


def _dot_x3(a, w_hi, w_lo):
    a_hi = a.astype(BF16)
    a_lo = (a - a_hi.astype(F32)).astype(BF16)
    return _dot(a_hi, w_hi) + (_dot(a_lo, w_hi) + _dot(a_hi, w_lo))


def _rms(x, w):
    return (x * lax.rsqrt(jnp.mean(x * x, axis=-1, keepdims=True) + EPS)) * w


def _sigmoid(x):
    return 1.0 / (1.0 + jnp.exp(-x))


def _rope(x, cos, sin_signed, half):
    n = x.shape[-1]
    lane = lax.broadcasted_iota(jnp.int32, x.shape, x.ndim - 1)
    first = (lane % (2 * half)) < half
    partner = jnp.where(first, pltpu.roll(x, n - half, x.ndim - 1), pltpu.roll(x, half, x.ndim - 1))
    return x * cos + partner * sin_signed


def _tile_lanes(a, reps):
    return a if reps == 1 else jnp.concatenate([a] * reps, axis=-1)


def _adaln_kernel(c_ref, w_ref, b_ref, o_ref):
    c = c_ref[...]
    a = c * _sigmoid(c)
    w = w_ref[0]
    w_hi = w.astype(BF16)
    w_lo = (w - w_hi.astype(F32)).astype(BF16)
    o_ref[0] = _dot_x3(a, w_hi, w_lo) + b_ref[0]


def _adaln(cond, ada_w, ada_b):
    depth, _, n6 = ada_w.shape
    rows = cond.shape[0]
    tn = 1024
    return pl.pallas_call(
        _adaln_kernel,
        grid=(depth, n6 // tn),
        in_specs=[
            pl.BlockSpec((rows, D_MODEL), lambda l, j: (0, 0)),
            pl.BlockSpec((1, D_MODEL, tn), lambda l, j: (l, 0, j)),
            pl.BlockSpec((1, 1, tn), lambda l, j: (l, 0, j)),
        ],
        out_specs=pl.BlockSpec((1, rows, tn), lambda l, j: (l, 0, j)),
        out_shape=jax.ShapeDtypeStruct((depth, rows, n6), F32),
        compiler_params=_cparams(2),
        name="adaln",
    )(cond, ada_w, ada_b.reshape(depth, 1, n6))


def _grp_spec(tm, width):
    return pl.BlockSpec((1, tm, width), lambda g, i: (g, i, 0))


ROW_PARTS = 2


def _row_part_pipeline(n_rows, before, matmul, after):
    n = n_rows // ROW_PARTS
    parts = [pl.ds(i * n, n) for i in range(ROW_PARTS)]
    a = before(parts[0])
    for i, rows in enumerate(parts):
        b = matmul(rows, a)
        if i + 1 < ROW_PARTS:
            a = before(parts[i + 1])
        after(rows, b)


ROW_TILE = (D_MODEL // LANES, LANES)


def _row_tiles(x):
    return x.reshape((x.shape[0],) + ROW_TILE)


def _row_tile_spec(tm, first_group=0):
    return pl.BlockSpec((1, tm) + ROW_TILE, lambda g, i: (g + first_group, i, 0, 0))


def _mod_spec():
    return pl.BlockSpec((1, 1, D_MODEL), lambda g, i: (g, 0, 0))


def _const_spec(shape):
    nd = len(shape)
    return pl.BlockSpec(shape, lambda g, i: (0,) * nd)


def _two_source_specs(tm):
    last = GROUP_TOKENS // tm - 1
    xs = pl.BlockSpec((1, tm, D_MODEL), lambda g, i: (jnp.minimum(g, N_SAMPLE_GROUPS - 1),
                                                      jnp.where(g < N_SAMPLE_GROUPS, i, last), 0))
    xp = pl.BlockSpec((1, tm, D_MODEL), lambda g, i: (jnp.maximum(g - N_SAMPLE_GROUPS, 0),
                                                      jnp.where(g < N_SAMPLE_GROUPS, 0, i), 0))
    return xs, xp


def _rope_spec(tm):
    return pl.BlockSpec((1, tm, LANES), lambda g, i: (jnp.where(g < N_SAMPLE_GROUPS, 0, 1), i, 0))


def _rope_tables(d):
    half = d // 4
    t = np.arange(GROUP_TOKENS)
    pos = np.stack([(t // GRID_W).astype(np.float32), (t % GRID_W).astype(np.float32)], axis=1)
    lane = np.arange(d)
    axis = lane // (d // 2)
    j = lane % half
    first = (lane % (d // 2)) < half
    inv_freq = (np.float32(ROPE_BASE) ** (-(np.arange(half, dtype=np.float32)) / np.float32(half))).astype(np.float32)
    ang = (pos[:, axis] * inv_freq[j][None, :]).astype(np.float32)
    cos = np.cos(ang.astype(np.float64)).astype(np.float32)
    sin = np.sin(ang.astype(np.float64)).astype(np.float32)
    sin = np.where(first[None, :], -sin, sin)
    return cos, sin


def _pad_tables(cos, sin):
    t, w = cos.shape
    cos_p = np.ones((t, LANES), np.float32)
    sin_p = np.zeros((t, LANES), np.float32)
    cos_p[:, :w] = cos
    sin_p[:, :w] = sin
    cos2 = np.stack([cos_p, np.ones_like(cos_p)])
    sin2 = np.stack([sin_p, np.zeros_like(sin_p)])
    return jnp.asarray(cos2), jnp.asarray(sin2)


EV_COLS = dict(q=(0, 512), k=(512, 640), v=(640, 768), qa=(768, 1024), ka=(1024, 1280), va=(1280, 1792),
               oa=(1792, 2304))
EV_MAIN = 2304


LOG2E = 1.4426950408889634
V_SLOT = LANES


def _in_even_kernel(xs_ref, xp_ref, nw_ref, sc_ref, sh_ref, w_ref, wgh_ref, wgl_ref, gb_ref, bd_ref, qnw_ref,
                    knw_ref, cos_ref, sin_ref, wv_ref, vone_ref,
                    q_ref, kb_ref, vb_ref, qa_ref, ka_ref, va_ref, oa_ref, ga_ref, k0_ref, k1_ref, vaug_ref):
    g = pl.program_id(0)

    def modulate(rows):
        x = jnp.where(g >= N_SAMPLE_GROUPS, xp_ref[0, rows, :], xs_ref[0, rows, :])
        return _rms(x, nw_ref[...]) * (1.0 + sc_ref[0]) + sh_ref[0]

    def project(xm):
        xb = xm.astype(BF16)
        out = {name: _dot(xb, w_ref[:, lo:hi]) for name, (lo, hi) in EV_COLS.items()}
        out["vaug"] = _dot(xb, wv_ref[...])
        out["gates"] = _dot_x3(xm, wgh_ref[...], wgl_ref[...])
        return out

    def finish(rows, p):
        cos = cos_ref[0, rows, :]
        sin = sin_ref[0, rows, :]
        q = p["q"]
        q = q * lax.rsqrt(_dot((q * q).astype(BF16), bd_ref[...]) + EPS) * qnw_ref[...]
        q = _rope(q, _tile_lanes(cos, 4), _tile_lanes(sin, 4), B_HEAD_DIM // 4)
        q_ref[0, rows, :] = (q * (B_HEAD_DIM ** -0.5 * LOG2E)).astype(BF16)
        k = p["k"]
        k = k * lax.rsqrt(_dot((k * k).astype(BF16), bd_ref[0:LANES, 0:LANES]) + EPS) * knw_ref[...]
        k = _rope(k, cos, sin, B_HEAD_DIM // 4)
        kb_ref[0, rows, :] = k
        k0_ref[0, rows, :] = k[:, :B_HEAD_DIM].astype(BF16)
        k1_ref[0, rows, :] = k[:, B_HEAD_DIM:].astype(BF16)
        vb_ref[0, rows, :] = p["v"]
        vaug_ref[0, rows, :] = (p["vaug"] + vone_ref[...]).astype(BF16)
        qa_ref[0, rows, :] = p["qa"].astype(BF16)
        ka_ref[0, rows, :] = (p["ka"] * (A_QK_DIM ** -0.5)).astype(BF16)
        va_ref[0, rows, :] = p["va"].astype(BF16)
        oa_ref[0, rows, :] = p["oa"].astype(BF16)
        ga_ref[0, rows, :] = p["gates"] + gb_ref[...]

    _row_part_pipeline(q_ref.shape[1], modulate, lambda rows, xm: project(xm), finish)


def _in_even(xs, xp, nw, sc, sh, w_main, wg_hi, wg_lo, gb, bd, qnw, knw, cos, sin, w_vaug, v_one):
    tm = TOKEN_TILE
    xs_spec, xp_spec = _two_source_specs(tm)
    widths = [(512, BF16), (128, F32), (128, F32), (256, BF16), (256, BF16), (512, BF16), (512, BF16), (128, F32),
              (B_HEAD_DIM, BF16), (B_HEAD_DIM, BF16), (B_KV_HEADS * V_SLOT, BF16)]
    return pl.pallas_call(
        _in_even_kernel,
        grid=(N_GROUPS, GROUP_TOKENS // tm),
        in_specs=[xs_spec, xp_spec, _const_spec((1, D_MODEL)), _mod_spec(), _mod_spec(),
                  _const_spec(w_main.shape), _const_spec(wg_hi.shape), _const_spec(wg_lo.shape),
                  _const_spec(gb.shape), _const_spec(bd.shape), _const_spec(qnw.shape), _const_spec(knw.shape),
                  _rope_spec(tm), _rope_spec(tm), _const_spec(w_vaug.shape), _const_spec(v_one.shape)],
        out_specs=[_grp_spec(tm, w) for w, _ in widths],
        out_shape=[jax.ShapeDtypeStruct((N_GROUPS, GROUP_TOKENS, w), dt) for w, dt in widths],
        compiler_params=_cparams(2),
        name="in_even",
    )(xs, xp, nw, sc, sh, w_main, wg_hi, wg_lo, gb, bd, qnw, knw, cos, sin, w_vaug, v_one)


ATT_CHUNK = 256


def _kv_source(k_ref, k_idx, v_ref, v_idx, n, first=0):
    return n, (lambda rows: k_ref[k_idx(rows)]), (lambda rows: v_ref[v_idx(rows)]), first


def _softmax_pv_heads(heads, dv):
    def chunks_of(sources):
        out = []
        for n, load_k, load_v, first in sources:
            step = min(n, ATT_CHUNK)
            out += [(load_k, load_v, pl.ds(first + c, step)) for c in range(0, n, step)]
        return out

    plans = [(q, chunks_of(sources)) for q, sources in heads]
    scores = [[] for _ in plans]
    tile_max = [None] * len(plans)
    accs = [None] * len(plans)

    def score_chunk(h, i):
        q, chunks = plans[h]
        if i >= len(chunks):
            return
        s = _dot_nt(q, chunks[i][0](chunks[i][2]))
        scores[h].append(s)
        for j in range(0, s.shape[1], LANES):
            t = s[:, j:j + LANES]
            tile_max[h] = t if tile_max[h] is None else jnp.maximum(tile_max[h], t)

    def value_chunk(h, i, m):
        _, chunks = plans[h]
        if i >= len(chunks):
            return
        pv = _dot(jnp.exp2(scores[h][i] - m).astype(BF16), chunks[i][1](chunks[i][2]))
        accs[h] = pv if accs[h] is None else accs[h] + pv

    n_chunks = max(len(c) for _, c in plans)
    for i in range(n_chunks):
        score_chunk(0, i)
    for h in range(len(plans)):
        m = jnp.max(tile_max[h], axis=-1, keepdims=True)
        for i in range(n_chunks):
            value_chunk(h, i, m)
            if h + 1 < len(plans):
                score_chunk(h + 1, i)
    return [acc[:, :dv] / acc[:, dv:dv + 1] for acc in accs]


def _attn_branches(g, emit):
    @pl.when(g < N_SAMPLE_GROUPS)
    def _():
        emit(True)

    @pl.when(g >= N_SAMPLE_GROUPS)
    def _():
        emit(False)


def _gqa_kernel(q_ref, k0_ref, k1_ref, v_ref, kc_ref, vc_ref, o_ref):
    own = pl.multiple_of(pl.program_id(1) * SEG, SEG)
    q = q_ref[0]
    pair = 2 * B_HEAD_DIM

    def sources_of(kv, k_ref, latent):
        vl = slice(kv * V_SLOT, (kv + 1) * V_SLOT)
        new = functools.partial(_kv_source, k_ref, lambda r: (0, r), v_ref, lambda r: (0, r, vl))
        if not latent:
            return [new(SEG, own)]
        return [_kv_source(kc_ref, lambda r: (0, kv, r), vc_ref, lambda r: (0, r, vl), PAST_LEN),
                new(GROUP_TOKENS)]

    def emit(latent):
        heads = []
        for kv, k_ref in enumerate((k0_ref, k1_ref)):
            sources = sources_of(kv, k_ref, latent)
            for half in range(B_HEADS // B_KV_HEADS // 2):
                lo = (kv * B_HEADS // B_KV_HEADS + 2 * half) * B_HEAD_DIM
                heads.append((jnp.concatenate([q[:, lo:lo + B_HEAD_DIM], q[:, lo + B_HEAD_DIM:lo + pair]], axis=0),
                              sources))
        outs = []
        for o in _softmax_pv_heads(heads, B_HEAD_DIM):
            outs += [o[:SEG], o[SEG:]]
        o_ref[0] = jnp.concatenate(outs, axis=-1).astype(BF16)

    _attn_branches(pl.program_id(0), emit)


def _gqa_attention(q, k0, k1, v_aug, kc, vc):
    cache = lambda nd: (lambda g, c: (jnp.minimum(g, N_SAMPLE_GROUPS - 1),) + (0,) * (nd - 1))
    whole = lambda w: pl.BlockSpec((1, GROUP_TOKENS, w), lambda g, c: (g, 0, 0))
    return pl.pallas_call(
        _gqa_kernel,
        grid=(N_GROUPS, SEGS_PER_GROUP),
        in_specs=[pl.BlockSpec((1, SEG, B_HEADS * B_HEAD_DIM), lambda g, c: (g, c, 0)),
                  whole(B_HEAD_DIM), whole(B_HEAD_DIM), whole(B_KV_HEADS * V_SLOT),
                  pl.BlockSpec((1, B_KV_HEADS, PAST_LEN, B_HEAD_DIM), cache(4)),
                  pl.BlockSpec((1, PAST_LEN, B_KV_HEADS * V_SLOT), cache(3))],
        out_specs=pl.BlockSpec((1, SEG, B_HEADS * B_HEAD_DIM), lambda g, c: (g, c, 0)),
        out_shape=jax.ShapeDtypeStruct((N_GROUPS, GROUP_TOKENS, B_HEADS * B_HEAD_DIM), BF16),
        compiler_params=_cparams(2),
        name="gqa_attention",
    )(q, k0, k1, v_aug, kc, vc)


MLA_PAIR = 4
MLA_ROWS = 2 * SEG


def _mla_kernel(q_ref, k_ref, v_ref, kc_ref, vc_ref, o_ref):
    tile = pl.multiple_of(pl.program_id(2) * MLA_ROWS, MLA_ROWS)
    q = q_ref[0]

    def sources_of(j, latent, seg):
        kl = slice(j * C_QPAD, (j + 1) * C_QPAD)
        vl = slice(j * V_SLOT, (j + 1) * V_SLOT)
        new = functools.partial(_kv_source, k_ref, lambda r: (0, r, kl), v_ref, lambda r: (0, r, vl))
        if not latent:
            return [new(SEG, tile + seg * SEG)]
        return [_kv_source(kc_ref, lambda r: (0, r, kl), vc_ref, lambda r: (0, r, vl), PAST_LEN),
                new(GROUP_TOKENS)]

    def emit(latent):
        cols = [slice(j * C_QPAD, (j + 1) * C_QPAD) for j in range(MLA_PAIR)]
        if latent:
            outs = _softmax_pv_heads([(q[:, cols[j]], sources_of(j, True, 0)) for j in range(MLA_PAIR)], C_V)
        else:
            segs = range(MLA_ROWS // SEG)
            parts = _softmax_pv_heads([(q[s * SEG:(s + 1) * SEG, cols[j]], sources_of(j, False, s))
                                       for j in range(MLA_PAIR) for s in segs], C_V)
            outs = [jnp.concatenate(parts[j * len(segs):(j + 1) * len(segs)], axis=0) for j in range(MLA_PAIR)]
        o_ref[0] = jnp.concatenate(outs, axis=-1).astype(BF16)

    _attn_branches(pl.program_id(0), emit)


def _mla_attention(q, k, v_aug, kc, vc):
    wq, wv = MLA_PAIR * C_QPAD, MLA_PAIR * V_SLOT
    cache = lambda g, p, c: (jnp.minimum(g, N_SAMPLE_GROUPS - 1), 0, p)
    return pl.pallas_call(
        _mla_kernel,
        grid=(N_GROUPS, C_HEADS // MLA_PAIR, GROUP_TOKENS // MLA_ROWS),
        in_specs=[pl.BlockSpec((1, MLA_ROWS, wq), lambda g, p, c: (g, c, p)),
                  pl.BlockSpec((1, GROUP_TOKENS, wq), lambda g, p, c: (g, 0, p)),
                  pl.BlockSpec((1, GROUP_TOKENS, wv), lambda g, p, c: (g, 0, p)),
                  pl.BlockSpec((1, PAST_LEN, wq), cache),
                  pl.BlockSpec((1, PAST_LEN, wv), cache)],
        out_specs=pl.BlockSpec((1, MLA_ROWS, MLA_PAIR * C_V), lambda g, p, c: (g, c, p)),
        out_shape=jax.ShapeDtypeStruct((N_GROUPS, GROUP_TOKENS, C_HEADS * C_V), BF16),
        compiler_params=_cparams(3),
        name="mla_attention",
    )(q, k, v_aug, kc, vc)


def _out_kernel(*refs, even):
    if even:
        (xs_ref, xp_ref, att_ref, h_ref, gate_ref, g1_ref, w_ref, mnw_ref, n2w_ref, sc_ref, sh_ref, rwh_ref,
         rwl_ref, x1_ref, xpk_ref, aff_ref) = refs
    else:
        (x_ref, att_ref, h_ref, gate_ref, g1_ref, w_ref, n2w_ref, sc_ref, sh_ref, rwh_ref, rwl_ref,
         x1_ref, xpk_ref, aff_ref) = refs
    half = w_ref.shape[0] // 2

    def mixer(rows):
        if even:
            h = h_ref[0, rows, :]
            mnw = mnw_ref[...]
            heads = [_rms(h[:, j * A_V_DIM:(j + 1) * A_V_DIM], mnw[:, j * A_V_DIM:(j + 1) * A_V_DIM])
                     for j in range(A_HEADS)]
            mix = jnp.concatenate(heads, axis=-1) * _sigmoid(gate_ref[0, rows, :].astype(F32))
        else:
            mix = h_ref[0, rows, :] * jax.nn.gelu(gate_ref[0, rows, :].astype(F32))
        return mix.astype(BF16)

    def project(rows, mix):
        return _dot(att_ref[0, rows, :], w_ref[0:half, :]) + _dot(mix, w_ref[half:, :])

    def finish(rows, out):
        if even:
            x = jnp.where(pl.program_id(0) >= N_SAMPLE_GROUPS, xp_ref[0, rows, :], xs_ref[0, rows, :])
        else:
            x = x_ref[0, rows, :]
        x1 = x + g1_ref[0] * out
        x1_ref[0, rows, :] = x1
        xm = _rms(x1, n2w_ref[...]) * (1.0 + sc_ref[0]) + sh_ref[0]
        xpk_ref[0, rows] = _row_tiles(xm)
        logits = _dot_x3(xm, rwh_ref[...], rwl_ref[...])
        lane = lax.broadcasted_iota(jnp.int32, logits.shape, 1)
        logits = jnp.where(lane < N_EXPERTS, logits, -jnp.inf)
        e = jnp.exp(logits - jnp.max(logits, axis=-1, keepdims=True))
        aff_ref[0, rows, :] = e / jnp.sum(e, axis=-1, keepdims=True)

    _row_part_pipeline(x1_ref.shape[1], mixer, project, finish)


def _out_proj(x_args, att, h, gate, g1, w, mnw, n2w, sc, sh, rw_hi, rw_lo, *, even):
    tm = TOKEN_TILE
    if even:
        x_specs = list(_two_source_specs(tm))
        extra, extra_specs = [mnw], [_const_spec(mnw.shape)]
    else:
        x_specs = [_grp_spec(tm, D_MODEL)]
        extra, extra_specs = [], []
    return pl.pallas_call(
        functools.partial(_out_kernel, even=even),
        grid=(N_GROUPS, GROUP_TOKENS // tm),
        in_specs=x_specs + [_grp_spec(tm, 512), _grp_spec(tm, 512), _grp_spec(tm, 512), _mod_spec(),
                            _const_spec(w.shape)] + extra_specs +
                 [_const_spec(n2w.shape), _mod_spec(), _mod_spec(), _const_spec(rw_hi.shape),
                  _const_spec(rw_lo.shape)],
        out_specs=[_grp_spec(tm, D_MODEL), _row_tile_spec(tm), _grp_spec(tm, LANES)],
        out_shape=[jax.ShapeDtypeStruct((N_GROUPS, GROUP_TOKENS, D_MODEL), F32),
                   jax.ShapeDtypeStruct((N_GROUPS, GROUP_TOKENS) + ROW_TILE, F32),
                   jax.ShapeDtypeStruct((N_GROUPS, GROUP_TOKENS, LANES), F32)],
        compiler_params=_cparams(2),
        name="out_even" if even else "out_odd",
    )(*x_args, att, h, gate, g1, w, *extra, n2w, sc, sh, rw_hi, rw_lo)


def _scan_flags():
    g = pl.program_id(0)
    c = pl.program_id(1)
    cb = SEGS_PER_GROUP - 1 - c
    per_seq = jnp.where(g >= N_SAMPLE_GROUPS, 1, SEGS_PER_GROUP)
    starts = ((c % per_seq) == 0, (cb % per_seq) == per_seq - 1)
    ends = ((c % per_seq) == per_seq - 1, (cb % per_seq) == 0)
    return c, cb, starts, ends


def _store_or_add(ref, chunk, value, first_touch):
    rows = pl.ds(pl.multiple_of(chunk * SEG, SEG), SEG)

    @pl.when(first_touch)
    def _():
        ref[0, rows, :] = value

    @pl.when(jnp.logical_not(first_touch))
    def _():
        ref[0, rows, :] += value


def _log_sigmoid(x):
    return jnp.minimum(x, 0.0) - jnp.log1p(jnp.exp(-jnp.abs(x)))


C_AUG = 2 * A_V_DIM


def _mlstm_kernel(qf_ref, kf_ref, vf_ref, gf_ref, qb_ref, kb_ref, vb_ref, gb_ref, s0_ref, m0_ref,
                  h_ref, sf_ref, sb_ref, mf_ref, mb_ref, c_scr, m_scr):
    c, cb, starts, _ = _scan_flags()
    L = SEG
    row = lax.broadcasted_iota(jnp.int32, (L, L), 0)
    col = lax.broadcasted_iota(jnp.int32, (L, L), 1)
    lane = lax.broadcasted_iota(jnp.int32, (1, LANES), 1)
    ones_col = (lax.broadcasted_iota(jnp.int32, (L, A_V_DIM), 1) == 0).astype(BF16)

    for d in range(2):
        @pl.when(starts[d])
        def _():
            c_scr[d] = s0_ref[0, d]
            m_scr[d] = m0_ref[0, d]

    dirs = []
    for d, (q_ref, k_ref, v_ref, g_ref) in enumerate(((qf_ref, kf_ref, vf_ref, gf_ref),
                                                       (qb_ref, kb_ref, vb_ref, gb_ref))):
        mask = (col <= row) if d == 0 else (col >= row)
        tri = mask.astype(BF16)
        gates = g_ref[0]
        lf = _log_sigmoid(gates)
        hi, mid, lo = _split3(lf)
        cum = _dot(tri, hi) + (_dot(tri, mid) + _dot(tri, lo))
        dirs.append(dict(mask=mask, gates=gates, cum=cum, total=jnp.sum(lf, axis=0, keepdims=True),
                         q=q_ref[0], k=k_ref[0], v=v_ref[0], m_vec=m_scr[d]))
    for dd in dirs:
        dd["gates_t"] = dd["gates"].T
        dd["cum_t"] = dd["cum"].T
        dd["k_t"] = dd["k"].astype(F32).T

    chains = [dict(d=d, h=h) for d in range(2) for h in range(A_HEADS)]
    for ch in chains:
        d, h, dd = ch["d"], ch["h"], dirs[ch["d"]]
        ci, cf = d * 2 * A_HEADS + h, d * 2 * A_HEADS + A_HEADS + h
        ch["bc"], ch["br"] = dd["cum"][:, cf:cf + 1], dd["cum_t"][cf:cf + 1, :]
        ch["ir"] = dd["gates_t"][ci:ci + 1, :]
        ch["tot"] = dd["total"][:, cf:cf + 1]
        ch["m_prev"] = dd["m_vec"][:, h:h + 1]
        ch["qh"] = dd["q"][:, h * A_QK_DIM:(h + 1) * A_QK_DIM]
        ch["kh"] = dd["k"][:, h * A_QK_DIM:(h + 1) * A_QK_DIM]
        ch["v_aug"] = jnp.concatenate([dd["v"][:, h * A_V_DIM:(h + 1) * A_V_DIM], ones_col], axis=-1)
        ch["state"] = c_scr[d, h]
    for ch in chains:
        ch["qk"] = _dot_nt(ch["qh"], ch["kh"])
        ch["qc"] = _dot(ch["qh"], ch["state"].astype(BF16))
    for ch in chains:
        ch["dmat"] = jnp.where(dirs[ch["d"]]["mask"], ch["bc"] - ch["br"] + ch["ir"], -jnp.inf)
        ch["m_inter"] = ch["bc"] + ch["m_prev"]
        ch["g_row"] = ch["tot"] - ch["br"] + ch["ir"]
    for ch in chains:
        ch["m_t"] = jnp.maximum(ch["m_inter"], jnp.max(ch["dmat"], axis=-1, keepdims=True))
        ch["m_new"] = jnp.maximum(ch["tot"] + ch["m_prev"], jnp.max(ch["g_row"], axis=-1, keepdims=True))
    for ch in chains:
        ch["s"] = ch["qk"] * jnp.exp(ch["dmat"] - ch["m_t"])
        ch["w_inter"] = jnp.exp(ch["m_inter"] - ch["m_t"])
        ch["kw"] = (dirs[ch["d"]]["k_t"][ch["h"] * A_QK_DIM:(ch["h"] + 1) * A_QK_DIM, :]
                    * jnp.exp(ch["g_row"] - ch["m_new"])).astype(BF16)
    for ch in chains:
        ch["sv"] = _dot(ch["s"].astype(BF16), ch["v_aug"])
        ch["dstate"] = _dot(ch["kw"], ch["v_aug"])
    for ch in chains:
        num = ch["w_inter"] * ch["qc"][:, :A_V_DIM] + ch["sv"][:, :A_V_DIM]
        den = ch["w_inter"] * ch["qc"][:, A_V_DIM:A_V_DIM + 1] + jnp.sum(ch["s"], axis=-1, keepdims=True)
        ch["out"] = num / jnp.maximum(jnp.abs(den), jnp.exp(-ch["m_t"]))
        decay = jnp.exp(ch["tot"] + ch["m_prev"] - ch["m_new"])
        c_scr[ch["d"], ch["h"]] = decay * ch["state"] + ch["dstate"]
    for d in range(2):
        m_out = dirs[d]["m_vec"]
        for ch in chains[d * A_HEADS:(d + 1) * A_HEADS]:
            m_out = jnp.where(lane == ch["h"], ch["m_new"], m_out)
        m_scr[d] = m_out
    h_f = jnp.concatenate([ch["out"] for ch in chains[:A_HEADS]], axis=-1)
    h_b = jnp.concatenate([ch["out"] for ch in chains[A_HEADS:]], axis=-1)
    _store_or_add(h_ref, c, h_f, c < cb)
    _store_or_add(h_ref, cb, h_b, c < cb)
    sf_ref[0, 0] = c_scr[0]
    sb_ref[0, 0] = c_scr[1]
    mf_ref[0, 0] = m_scr[0]
    mb_ref[0, 0] = m_scr[1]


def _mlstm(qa, ka, va, gates, s0, m0):
    fwd = lambda w: pl.BlockSpec((1, SEG, w), lambda g, c: (g, c, 0))
    bwd = lambda w: pl.BlockSpec((1, SEG, w), lambda g, c: (g, SEGS_PER_GROUP - 1 - c, 0))
    st_shape = (1, 1, A_HEADS, A_QK_DIM, C_AUG)
    st_f = pl.BlockSpec(st_shape, lambda g, c: (g, c, 0, 0, 0))
    st_b = pl.BlockSpec(st_shape, lambda g, c: (g, SEGS_PER_GROUP - 1 - c, 0, 0, 0))
    m_f = pl.BlockSpec((1, 1, 1, LANES), lambda g, c: (g, c, 0, 0))
    m_b = pl.BlockSpec((1, 1, 1, LANES), lambda g, c: (g, SEGS_PER_GROUP - 1 - c, 0, 0))
    widths = (A_HEADS * A_QK_DIM, A_HEADS * A_QK_DIM, A_HEADS * A_V_DIM, LANES)
    st_out = jax.ShapeDtypeStruct((N_GROUPS, SEGS_PER_GROUP, A_HEADS, A_QK_DIM, C_AUG), F32)
    m_out = jax.ShapeDtypeStruct((N_GROUPS, SEGS_PER_GROUP, 1, LANES), F32)
    return pl.pallas_call(
        _mlstm_kernel,
        grid=(N_GROUPS, SEGS_PER_GROUP),
        in_specs=[fwd(w) for w in widths] + [bwd(w) for w in widths] + [
            pl.BlockSpec((1, 2, A_HEADS, A_QK_DIM, C_AUG), lambda g, c: (g, 0, 0, 0, 0)),
            pl.BlockSpec((1, 2, 1, LANES), lambda g, c: (g, 0, 0, 0))],
        out_specs=[pl.BlockSpec((1, GROUP_TOKENS, A_HEADS * A_V_DIM), lambda g, c: (g, 0, 0)),
                   st_f, st_b, m_f, m_b],
        out_shape=[jax.ShapeDtypeStruct((N_GROUPS, GROUP_TOKENS, A_HEADS * A_V_DIM), F32),
                   st_out, st_out, m_out, m_out],
        scratch_shapes=[pltpu.VMEM((2, A_HEADS, A_QK_DIM, C_AUG), F32), pltpu.VMEM((2, 1, LANES), F32)],
        compiler_params=_cparams(2),
        name="mlstm",
    )(qa, ka, va, gates, qa, ka, va, gates, s0, m0)


def _one_minus_square_of_exp(log_a):
    a = jnp.exp(log_a)
    return a, jnp.tanh(-log_a) * (1.0 + a * a)


def _linear_scan(a, b, h_in, reverse):
    n = a.shape[0] // SUBLANES
    a3 = a.reshape(n, SUBLANES, D_RNN)
    b3 = b.reshape(n, SUBLANES, D_RNN)
    row = lax.broadcasted_iota(jnp.int32, a3.shape, 1)
    for d in (1, 2, 4):
        shift = SUBLANES - d if reverse else d
        valid = (row < SUBLANES - d) if reverse else (row >= d)
        b3 = jnp.where(valid, a3 * pltpu.roll(b3, shift, 1) + b3, b3)
        a3 = jnp.where(valid, a3 * pltpu.roll(a3, shift, 1), a3)
    h = h_in
    outs = [None] * n
    for j in (reversed(range(n)) if reverse else range(n)):
        hj = a3[j] * h + b3[j]
        outs[j] = hj
        h = hj[0:1] if reverse else hj[SUBLANES - 1:SUBLANES]
    return jnp.concatenate(outs, axis=0), h


def _rglru_kernel(xf_ref, pf_ref, nf_ref, xb_ref, pb_ref, nb_ref, cw_ref, cb_ref, wa_ref, wx_ref, ba_ref,
                  bx_ref, lam_ref, h0_ref, o_ref, sf_ref, sb_ref, h_scr):
    c, cb, starts, ends = _scan_flags()
    cw = cw_ref[...]

    def run(d, x_ref, p_ref, n_ref, chunk, start, end, first_touch):
        @pl.when(start if d == 0 else end)
        def _():
            h_scr[d] = h0_ref[0, d]

        prev = jnp.where(start, 0.0, p_ref[0])
        nxt = jnp.where(end, 0.0, n_ref[0])
        xcat = jnp.concatenate([prev, x_ref[0], nxt], axis=0)
        xc = cb_ref[...]
        for j in range(CONV_W):
            off = SUBLANES - CONV_LEFT + j
            xc = xc + cw[j:j + 1, :] * xcat[off:off + SEG, :]
        xcb = xc.astype(BF16)
        r = _sigmoid(_dot(xcb, wa_ref[d]) + ba_ref[d])
        i = _sigmoid(_dot(xcb, wx_ref[d]) + bx_ref[d])
        lam = lam_ref[d]
        softplus_neg = jnp.maximum(-lam, 0.0) + jnp.log1p(jnp.exp(-jnp.abs(lam)))
        log_a = (-RG_C * r) * softplus_neg
        a, one_minus_a2 = _one_minus_square_of_exp(log_a)
        b = jnp.sqrt(one_minus_a2) * (i * xc)
        hs, h_last = _linear_scan(a, b, h_scr[d], reverse=(d == 1))
        h_scr[d] = h_last
        _store_or_add(o_ref, chunk, hs, first_touch)

    run(0, xf_ref, pf_ref, nf_ref, c, starts[0], ends[0], c < cb)
    run(1, xb_ref, pb_ref, nb_ref, cb, ends[1], starts[1], c < cb)
    sf_ref[0, 0] = h_scr[0]
    sb_ref[0, 0] = h_scr[1]


def _rglru(xr, conv_w, conv_b, wa, wx, ba, bx, lam, h0):
    blocks = SEG // SUBLANES
    n_blocks = GROUP_TOKENS // SUBLANES
    cbk = lambda c: SEGS_PER_GROUP - 1 - c
    x_f = pl.BlockSpec((1, SEG, D_RNN), lambda g, c: (g, c, 0))
    x_b = pl.BlockSpec((1, SEG, D_RNN), lambda g, c: (g, cbk(c), 0))
    halo = lambda f: pl.BlockSpec((1, SUBLANES, D_RNN), f)
    p_f = halo(lambda g, c: (g, jnp.maximum(c * blocks - 1, 0), 0))
    n_f = halo(lambda g, c: (g, jnp.minimum((c + 1) * blocks, n_blocks - 1), 0))
    p_b = halo(lambda g, c: (g, jnp.maximum(cbk(c) * blocks - 1, 0), 0))
    n_b = halo(lambda g, c: (g, jnp.minimum((cbk(c) + 1) * blocks, n_blocks - 1), 0))
    st = jax.ShapeDtypeStruct((N_GROUPS, SEGS_PER_GROUP, 1, D_RNN), F32)
    return pl.pallas_call(
        _rglru_kernel,
        grid=(N_GROUPS, SEGS_PER_GROUP),
        in_specs=[x_f, p_f, n_f, x_b, p_b, n_b, _const_spec(conv_w.shape), _const_spec(conv_b.shape),
                  _const_spec(wa.shape), _const_spec(wx.shape), _const_spec(ba.shape), _const_spec(bx.shape),
                  _const_spec(lam.shape), pl.BlockSpec((1, 2, 1, D_RNN), lambda g, c: (g, 0, 0, 0))],
        out_specs=[pl.BlockSpec((1, GROUP_TOKENS, D_RNN), lambda g, c: (g, 0, 0)),
                   pl.BlockSpec((1, 1, 1, D_RNN), lambda g, c: (g, c, 0, 0)),
                   pl.BlockSpec((1, 1, 1, D_RNN), lambda g, c: (g, cbk(c), 0, 0))],
        out_shape=[jax.ShapeDtypeStruct((N_GROUPS, GROUP_TOKENS, D_RNN), F32), st, st],
        scratch_shapes=[pltpu.VMEM((2, 1, D_RNN), F32)],
        compiler_params=_cparams(2),
        name="rglru",
    )(xr, xr, xr, xr, xr, xr, conv_w, conv_b, wa, wx, ba, bx, lam, h0)


OD_COLS = dict(cq=(0, 384), ckv=(384, 640), xr=(640, 1152), xg=(1152, 1664))
MLA_SCALE = (C_NOPE + C_ROPE) ** -0.5 * LOG2E


def _mla_keys(ckv_b, kr_b, wuk_ref, place_ref, wuv_ref):
    kcat = _dot(ckv_b, wuk_ref[...]) + _dot(kr_b, place_ref[...])
    v = _dot(ckv_b, wuv_ref[...])
    lane = lax.broadcasted_iota(jnp.int32, (1, v.shape[-1]), 1)
    return kcat.astype(BF16), (v + (lane % V_SLOT == C_V).astype(F32)).astype(BF16)


def _in_odd_kernel(x_ref, moe_ref, g2_ref, nw_ref, sc_ref, sh_ref, w_ref, wkr_ref, qnw_ref, kvnw_ref, wuq_ref,
                   wuk_ref, place_ref, wuv_ref, cq_ref, sq_ref, ck_ref, sk_ref,
                   x2_ref, q_ref, kcat_ref, v_ref, ckv_ref, kr_ref, xr_ref, xg_ref):
    def modulate(rows):
        moe = moe_ref[0, rows].reshape(rows.size, D_MODEL)
        x = x_ref[0, rows, :] + g2_ref[0] * moe
        x2_ref[0, rows, :] = x
        return (_rms(x, nw_ref[...]) * (1.0 + sc_ref[0]) + sh_ref[0]).astype(BF16)

    def project(rows, xb):
        out = {name: _dot(xb, w_ref[:, lo:hi]) for name, (lo, hi) in OD_COLS.items()}
        out["kr"] = _dot(xb, wkr_ref[...])
        return out

    def finish(rows, p):
        cq = _rms(p["cq"], qnw_ref[...])
        q = _dot(cq.astype(BF16), wuq_ref[...])
        q = _rope(q, _tile_lanes(cq_ref[0, rows, :], C_HEADS), _tile_lanes(sq_ref[0, rows, :], C_HEADS), C_ROPE // 4)
        q_ref[0, rows, :] = (q * MLA_SCALE).astype(BF16)
        ckv = _rms(p["ckv"], kvnw_ref[...])
        ckv_ref[0, rows, :] = ckv
        kr = _rope(p["kr"], ck_ref[0, rows, :], sk_ref[0, rows, :], C_ROPE // 4)
        kr_ref[0, rows, :] = kr
        kcat_ref[0, rows, :], v_ref[0, rows, :] = _mla_keys(ckv.astype(BF16), kr.astype(BF16), wuk_ref, place_ref,
                                                           wuv_ref)
        xr_ref[0, rows, :] = p["xr"]
        xg_ref[0, rows, :] = p["xg"].astype(BF16)

    _row_part_pipeline(x_ref.shape[1], modulate, project, finish)


def _in_odd(x, moe, g2, nw, sc, sh, w_main, w_kr, qnw, kvnw, w_uq, w_uk, place, w_uv, cos_q, sin_q, cos_k, sin_k):
    tm = TOKEN_TILE
    widths = [(D_MODEL, F32), (C_HEADS * C_QPAD, BF16), (C_HEADS * C_QPAD, BF16), (C_HEADS * V_SLOT, BF16),
              (C_KV_LORA, F32), (LANES, F32), (D_RNN, F32), (D_RNN, BF16)]
    consts = [nw, None, None, w_main, w_kr, qnw, kvnw, w_uq, w_uk, place, w_uv]
    const_specs = [_mod_spec() if a is None else _const_spec(a.shape) for a in consts]
    return pl.pallas_call(
        _in_odd_kernel,
        grid=(N_GROUPS, GROUP_TOKENS // tm),
        in_specs=[_grp_spec(tm, D_MODEL), _row_tile_spec(tm), _mod_spec()] + const_specs +
                 [_rope_spec(tm)] * 4,
        out_specs=[_grp_spec(tm, w) for w, _ in widths],
        out_shape=[jax.ShapeDtypeStruct((N_GROUPS, GROUP_TOKENS, w), dt) for w, dt in widths],
        compiler_params=_cparams(2),
        name="in_odd",
    )(x, moe, g2, nw, sc, sh, w_main, w_kr, qnw, kvnw, w_uq, w_uk, place, w_uv, cos_q, sin_q, cos_k, sin_k)


def _kv_expand_kernel(ckv_ref, kr_ref, wuk_ref, place_ref, wuv_ref, kcat_ref, v_ref):
    kcat_ref[0], v_ref[0] = _mla_keys(ckv_ref[0].astype(BF16), kr_ref[0].astype(BF16), wuk_ref, place_ref, wuv_ref)


def _kv_expand(ckv, kr_pad, w_uk, place, w_uv):
    n = ckv.shape[0]
    cst = lambda a: pl.BlockSpec(a.shape, lambda b: (0,) * a.ndim)
    return pl.pallas_call(
        _kv_expand_kernel,
        grid=(n,),
        in_specs=[pl.BlockSpec((1, PAST_LEN, C_KV_LORA), lambda b: (b, 0, 0)),
                  pl.BlockSpec((1, PAST_LEN, LANES), lambda b: (b, 0, 0)), cst(w_uk), cst(place), cst(w_uv)],
        out_specs=[pl.BlockSpec((1, PAST_LEN, C_HEADS * C_QPAD), lambda b: (b, 0, 0)),
                   pl.BlockSpec((1, PAST_LEN, C_HEADS * V_SLOT), lambda b: (b, 0, 0))],
        out_shape=[jax.ShapeDtypeStruct((n, PAST_LEN, C_HEADS * C_QPAD), BF16),
                   jax.ShapeDtypeStruct((n, PAST_LEN, C_HEADS * V_SLOT), BF16)],
        compiler_params=_cparams(1),
        name="kv_expand",
    )(ckv, kr_pad, w_uk, place, w_uv)


def _route_kernel(aff_ref, idx_ref, cum_scr):
    is_ctx = pl.program_id(0) >= N_SAMPLE_GROUPS
    cap = jnp.where(is_ctx, CTX_CAP, SLOTS).astype(F32)
    aff = aff_ref[0].reshape(SEGS_PER_GROUP, SEG, LANES)

    def per_set(per_seg, combine):
        whole = jnp.broadcast_to(combine(per_seg, axis=0, keepdims=True), per_seg.shape)
        return jnp.where(is_ctx, per_seg, whole)

    def count(mask):
        return per_set(jnp.sum(mask.astype(F32), axis=1), jnp.sum)

    def as_float(word):
        return lax.bitcast_convert_type(word, F32)[:, None, :]

    def search(i, word):
        cand = word | lax.shift_left(jnp.int32(1), 30 - i)
        return jnp.where(count(aff >= as_float(cand)) >= cap, cand, word)

    word = lax.fori_loop(0, 31, search, jnp.zeros((SEGS_PER_GROUP, LANES), jnp.int32))
    upper = as_float(word + 1)
    kth = as_float(word)
    left = cap - count(aff >= upper)
    found = jnp.zeros(left.shape, jnp.bool_)
    for _ in range(3):
        cand = per_set(jnp.max(jnp.where(aff < upper, aff, -1.0), axis=1), jnp.max)[:, None, :]
        n_cand = count(aff == cand)
        hit = jnp.logical_and(jnp.logical_not(found), left <= n_cand)
        kth = jnp.where(hit[:, None, :], cand, kth)
        found = jnp.logical_or(found, hit)
        left = jnp.where(found, left, left - n_cand)
        upper = jnp.where(found[:, None, :], upper, cand)
    above = aff > kth
    tied = aff == kth
    need = cap - count(above)

    tri = (lax.broadcasted_iota(jnp.int32, (SEG, SEG), 1) <= lax.broadcasted_iota(jnp.int32, (SEG, SEG), 0)).astype(BF16)

    def prefix(mask, across_segments):
        m = mask.astype(BF16)
        outs = []
        offset = jnp.zeros((1, LANES), F32)
        for s in range(SEGS_PER_GROUP):
            p = _dot(tri, m[s])
            outs.append(p + jnp.where(across_segments, offset, 0.0))
            offset = offset + p[SEG - 1:SEG, :]
        return jnp.stack(outs)

    tied_rank = prefix(tied, jnp.logical_not(is_ctx)) - tied.astype(F32)
    keep = above | (tied & (tied_rank < need[:, None, :]))
    cum = prefix(keep, True)
    for s in range(SEGS_PER_GROUP):
        cum_scr[s] = cum[s].T

    part = SLOTS // 2
    lane = lax.broadcasted_iota(jnp.int32, (part, LANES), 1)
    parts = []
    for p in range(SLOTS // part):
        slot = (lax.broadcasted_iota(jnp.int32, (part, LANES), 0) + p * part).astype(F32)
        columns = jnp.zeros((part, LANES), F32)
        for e in range(N_EXPERTS):
            def block(s, acc):
                for half in range(SEG // LANES):
                    acc = acc + (cum_scr[s, e:e + 1, half * LANES:(half + 1) * LANES] <= slot).astype(F32)
                return acc
            counts = lax.fori_loop(0, SEGS_PER_GROUP, block, jnp.zeros((part, LANES), F32))
            columns = jnp.where(lane == e, jnp.sum(counts, axis=-1, keepdims=True), columns)
        parts.append(columns)
    columns = jnp.concatenate(parts, axis=0)
    rows = jnp.minimum(columns, GROUP_TOKENS - 1.0).T
    idx_ref[0] = rows[:N_EXPERTS].astype(jnp.int32)


def _route(aff):
    return pl.pallas_call(
        _route_kernel,
        grid=(N_GROUPS,),
        in_specs=[pl.BlockSpec((1, GROUP_TOKENS, LANES), lambda g: (g, 0, 0))],
        out_specs=pl.BlockSpec((1, N_EXPERTS, SLOTS), lambda g: (g, 0, 0)),
        out_shape=jax.ShapeDtypeStruct((N_GROUPS, N_EXPERTS, SLOTS), jnp.int32),
        scratch_shapes=[pltpu.VMEM((SEGS_PER_GROUP, LANES, SEG), F32)],
        compiler_params=_cparams(1),
        name="route",
    )(aff)


SCATTER_BATCH = 8


def _moe_kernel(idx_ref, xpk_ref, aff_ref, wg_ref, wu_ref, wd_ref, o_ref, xe_a, xe_b, ga_a, ga_b, ye_a, ye_b):
    g = pl.program_id(0)
    e = pl.program_id(1)
    base = (g * N_EXPERTS + e) * SLOTS

    def gather(first, xe, ga):
        for s in range(SLOTS):
            t = idx_ref[first + s]
            xe[s] = xpk_ref[0, t]
            ga[pl.ds(s, 1), :] = aff_ref[0, pl.ds(t, 1), :]

    def scatter(first, ye):
        for s0 in range(0, SLOTS, SCATTER_BATCH):
            toks = [idx_ref[first + s0 + k] for k in range(SCATTER_BATCH)]
            tiles = [o_ref[0, t] + ye[s0 + k] for k, t in enumerate(toks)]
            for t, tile in zip(toks, tiles):
                o_ref[0, t] = tile

    @pl.when(e == 0)
    def _():
        o_ref[...] = jnp.zeros_like(o_ref)
        ye_b[...] = jnp.zeros_like(ye_b)
        gather(base, xe_a, ga_a)

    def step(xe_cur, ga_cur, ye_cur, xe_next, ga_next, ye_prev):
        gather(base + jnp.where(e < N_EXPERTS - 1, SLOTS, 0), xe_next, ga_next)
        xe = xe_cur[...].reshape(SLOTS, D_MODEL).astype(BF16)
        lane = lax.broadcasted_iota(jnp.int32, (SLOTS, LANES), 1)
        gate = jnp.sum(jnp.where(lane == e, ga_cur[...], 0.0), axis=-1, keepdims=True)
        a = _dot(xe, wg_ref[0])
        hid = (a * _sigmoid(a)) * _dot(xe, wu_ref[0])
        ye = _row_tiles(_dot(hid.astype(BF16), wd_ref[0]) * gate)
        scatter(base - jnp.where(e > 0, SLOTS, 0), ye_prev)
        ye_cur[...] = ye

    @pl.when(e % 2 == 0)
    def _():
        step(xe_a, ga_a, ye_a, xe_b, ga_b, ye_b)

    @pl.when(e % 2 == 1)
    def _():
        step(xe_b, ga_b, ye_b, xe_a, ga_a, ye_a)

    @pl.when(e == N_EXPERTS - 1)
    def _():
        scatter(base, ye_b if (N_EXPERTS - 1) % 2 else ye_a)


def _moe(idx, xpk, aff, w_gate, w_up, w_down):
    once = pl.Buffered(1)
    return pl.pallas_call(
        _moe_kernel,
        grid_spec=pltpu.PrefetchScalarGridSpec(
            num_scalar_prefetch=1,
            grid=(N_GROUPS, N_EXPERTS),
            in_specs=[
                pl.BlockSpec((1, GROUP_TOKENS) + ROW_TILE, lambda g, e, idx: (g, 0, 0, 0), pipeline_mode=once),
                pl.BlockSpec((1, GROUP_TOKENS, LANES), lambda g, e, idx: (g, 0, 0), pipeline_mode=once),
                pl.BlockSpec((1, D_MODEL, EXPERT_FF), lambda g, e, idx: (e, 0, 0)),
                pl.BlockSpec((1, D_MODEL, EXPERT_FF), lambda g, e, idx: (e, 0, 0)),
                pl.BlockSpec((1, EXPERT_FF, D_MODEL), lambda g, e, idx: (e, 0, 0)),
            ],
            out_specs=pl.BlockSpec((1, GROUP_TOKENS) + ROW_TILE, lambda g, e, idx: (g, 0, 0, 0),
                                   pipeline_mode=once),
            scratch_shapes=[pltpu.VMEM((SLOTS,) + ROW_TILE, F32), pltpu.VMEM((SLOTS,) + ROW_TILE, F32),
                            pltpu.VMEM((SLOTS, LANES), F32), pltpu.VMEM((SLOTS, LANES), F32),
                            pltpu.VMEM((SLOTS,) + ROW_TILE, F32), pltpu.VMEM((SLOTS,) + ROW_TILE, F32)],
        ),
        out_shape=jax.ShapeDtypeStruct((N_GROUPS, GROUP_TOKENS) + ROW_TILE, F32),
        compiler_params=_cparams(2),
        name="moe",
    )(idx.reshape(-1), xpk, aff, w_gate, w_up, w_down)


def _final_kernel(x_ref, moe_ref, g2_ref, nw_ref, o_ref):
    o_ref[0] = _rms(x_ref[0] + g2_ref[0] * moe_ref[0].reshape(x_ref.shape[1:]), nw_ref[...])


def _final(x, moe, g2, nw, first_group, n_groups):
    tm = TOKEN_TILE
    src = lambda w: pl.BlockSpec((1, tm, w), lambda g, i: (g + first_group, i, 0))
    return pl.pallas_call(
        _final_kernel,
        grid=(n_groups, GROUP_TOKENS // tm),
        in_specs=[src(D_MODEL), _row_tile_spec(tm, first_group),
                  pl.BlockSpec((1, 1, D_MODEL), lambda g, i: (g + first_group, 0, 0)),
                  _const_spec(nw.shape)],
        out_specs=_grp_spec(tm, D_MODEL),
        out_shape=jax.ShapeDtypeStruct((n_groups, GROUP_TOKENS, D_MODEL), F32),
        compiler_params=_cparams(2),
        name="final_norm",
    )(x, moe, g2, nw)


def _hi_lo(w):
    hi = w.astype(BF16)
    return hi, (w - hi.astype(F32)).astype(BF16)


def _pad_cols(w, width=LANES):
    return jnp.pad(w, ((0, 0),) * (w.ndim - 1) + ((0, width - w.shape[-1]),))


def _ctx_only(a):
    tail = a[N_SAMPLE_GROUPS:]
    return tail.reshape((N_CTX_GROUPS * SEGS_PER_GROUP,) + tail.shape[2:])


def _with_ctx_zeros(a):
    return jnp.concatenate([a, jnp.zeros((N_CTX_GROUPS,) + a.shape[1:], a.dtype)], axis=0)


def _moe_block(xpk, aff, w_gate, w_up, w_down):
    idx = _route(aff)
    return _moe(idx, xpk, aff, w_gate.astype(BF16), w_up.astype(BF16), w_down.astype(BF16))


def kernel(x_prompt, x_sample, state_mlstm_C, state_mlstm_n, state_mlstm_m, cache_gqa_k, cache_gqa_v, cache_mla_ckv, cache_mla_krope, state_rglru_h, c, c_ctx, norm1_w, norm2_w, final_norm_w, ada_w, ada_b, ev_w_in, ev_w_out, gqa_q_norm, gqa_k_norm, mlstm_gate_b, mlstm_norm_w, od_w_in, od_w_out, mla_q_norm, mla_kv_norm, mla_w_uq, mla_w_ukv, rg_conv_w, rg_conv_b, rg_wa, rg_ba, rg_wx, rg_bx, rg_lambda, router_w, exp_w_gate, exp_w_up, exp_w_down):
    n_ctx = x_prompt.shape[0]
    xs = x_sample
    xp = x_prompt.reshape(N_CTX_GROUPS, GROUP_TOKENS, D_MODEL)

    cond = jnp.concatenate([c, jnp.broadcast_to(c_ctx[None, :], (16 - N_SAMPLE_GROUPS, D_MODEL))], axis=0)
    mods = _adaln(cond, ada_w, ada_b)[:, :N_GROUPS]
    mod = [[m[:, None, :] for m in jnp.split(mods[layer], 6, axis=-1)] for layer in range(2)]
    router = [_hi_lo(_pad_cols(router_w[layer])) for layer in range(2)]

    sh1, sc1, g1, sh2, sc2, g2 = mod[0]
    w_in = ev_w_in[0]
    wg_hi, wg_lo = _hi_lo(_pad_cols(w_in[:, EV_MAIN:]))
    bd = jnp.asarray(np.kron(np.eye(B_HEADS), np.full((B_HEAD_DIM, B_HEAD_DIM), 1.0 / B_HEAD_DIM)), BF16)
    cos_b, sin_b = _rope_tables(B_HEAD_DIM)
    cos_b, sin_b = _pad_tables(np.tile(cos_b, (1, 2)), np.tile(sin_b, (1, 2)))
    v_lo, v_hi = EV_COLS["v"]
    w_vaug = jnp.pad(w_in[:, v_lo:v_hi].reshape(D_MODEL, B_KV_HEADS, B_HEAD_DIM),
                     ((0, 0), (0, 0), (0, V_SLOT - B_HEAD_DIM))).reshape(D_MODEL, -1).astype(BF16)
    slot_one = np.zeros((1, B_KV_HEADS, V_SLOT), np.float32)
    slot_one[:, :, B_HEAD_DIM] = 1.0
    q, kb, vb, qa, ka, va, oa, ga, k0, k1, v_aug = _in_even(
        xs, xp, norm1_w[0][None, :], sc1, sh1, w_in[:, :EV_MAIN].astype(BF16), wg_hi, wg_lo,
        _pad_cols(mlstm_gate_b[0][None, :]), bd, jnp.tile(gqa_q_norm[0], B_HEADS)[None, :],
        jnp.tile(gqa_k_norm[0], B_KV_HEADS)[None, :], cos_b, sin_b, w_vaug,
        jnp.asarray(slot_one.reshape(1, -1)))
    kc = jnp.swapaxes(cache_gqa_k[:, 0], 1, 2).astype(BF16)
    vc = jnp.pad(cache_gqa_v[:, 0], ((0, 0), (0, 0), (0, 0), (0, V_SLOT - B_HEAD_DIM))) + jnp.asarray(slot_one)
    att = _gqa_attention(q, k0, k1, v_aug, kc, vc.reshape(N_SAMPLE_GROUPS, PAST_LEN, -1).astype(BF16))
    s0 = jnp.concatenate([state_mlstm_C[:, 0], state_mlstm_n[:, 0][..., None],
                          jnp.zeros(state_mlstm_C[:, 0].shape[:-1] + (C_AUG - A_V_DIM - 1,), F32)], axis=-1)
    m0 = _pad_cols(state_mlstm_m[:, 0])[:, :, None, :]
    h_a, st_f, st_b, m_f, m_b = _mlstm(qa, ka, va, ga, _with_ctx_zeros(s0), _with_ctx_zeros(m0))
    x1, xpk, aff = _out_proj((xs, xp), att, h_a, oa, g1, ev_w_out[0].astype(BF16), mlstm_norm_w[0][None, :],
                             norm2_w[0][None, :], sc2, sh2, *router[0], even=True)
    moe0 = _moe_block(xpk, aff, exp_w_gate[0], exp_w_up[0], exp_w_down[0])
    g2_0 = g2

    sh1, sc1, g1, sh2, sc2, g2 = mod[1]
    w_in = od_w_in[0]
    o_kr = C_Q_LORA + C_KV_LORA
    w_main = jnp.concatenate([w_in[:, :o_kr], w_in[:, o_kr + C_ROPE:]], axis=1).astype(BF16)
    w_kr = _pad_cols(w_in[:, o_kr:o_kr + C_ROPE]).astype(BF16)
    uq = mla_w_uq[0].reshape(C_Q_LORA, C_HEADS, C_NOPE + C_ROPE)
    w_uq = jnp.pad(uq, ((0, 0), (0, 0), (0, C_QPAD - C_NOPE - C_ROPE))).reshape(C_Q_LORA, -1).astype(BF16)
    ukv = mla_w_ukv[0].reshape(C_KV_LORA, C_HEADS, C_NOPE + C_V)
    w_uk = jnp.pad(ukv[..., :C_NOPE], ((0, 0), (0, 0), (0, C_QPAD - C_NOPE))).reshape(C_KV_LORA, -1).astype(BF16)
    w_uv = jnp.pad(ukv[..., C_NOPE:], ((0, 0), (0, 0), (0, V_SLOT - C_V))).reshape(C_KV_LORA, -1).astype(BF16)
    place_np = np.zeros((LANES, C_HEADS, C_QPAD), np.float32)
    for j in range(C_ROPE):
        place_np[j, :, C_NOPE + j] = 1.0
    place = jnp.asarray(place_np.reshape(LANES, -1), BF16)
    cos_c, sin_c = _rope_tables(C_ROPE)
    cos_k, sin_k = _pad_tables(cos_c, sin_c)
    cos_q = np.ones((GROUP_TOKENS, C_NOPE + C_ROPE), np.float32)
    sin_q = np.zeros((GROUP_TOKENS, C_NOPE + C_ROPE), np.float32)
    cos_q[:, C_NOPE:] = cos_c
    sin_q[:, C_NOPE:] = sin_c
    cos_q, sin_q = _pad_tables(cos_q, sin_q)
    x2, q, kcat, v, ckv, kr, xr, xg = _in_odd(
        x1, moe0, g2_0, norm1_w[1][None, :], sc1, sh1, w_main, w_kr, mla_q_norm[0][None, :],
        mla_kv_norm[0][None, :], w_uq, w_uk, place, w_uv, cos_q, sin_q, cos_k, sin_k)
    kcat_c, v_c = _kv_expand(cache_mla_ckv[:, 0], _pad_cols(cache_mla_krope[:, 0].reshape(-1, C_ROPE)).reshape(
        N_SAMPLE_GROUPS, PAST_LEN, LANES), w_uk, place, w_uv)
    att = _mla_attention(q, kcat, v, kcat_c, v_c)
    eye = jnp.eye(RG_BLOCKS, dtype=F32)
    dense = lambda w: jnp.einsum("knde,nm->kndme", w, eye).reshape(2, D_RNN, D_RNN).astype(BF16)
    h0 = _with_ctx_zeros(state_rglru_h[:, 0][:, :, None, :])
    rg, rs_f, rs_b = _rglru(xr, rg_conv_w[0], rg_conv_b[0][None, :], dense(rg_wa[0]), dense(rg_wx[0]),
                            rg_ba[0][:, None, :], rg_bx[0][:, None, :], rg_lambda[0][:, None, :], h0)
    x3, xpk, aff = _out_proj((x2,), att, rg, xg, g1, od_w_out[0].astype(BF16), None, norm2_w[1][None, :],
                             sc2, sh2, *router[1], even=False)
    moe1 = _moe_block(xpk, aff, exp_w_gate[1], exp_w_up[1], exp_w_down[1])

    fnw = final_norm_w[None, :]
    y_sample = _final(x3, moe1, g2, fnw, 0, N_SAMPLE_GROUPS)
    y_prompt = _final(x3, moe1, g2, fnw, N_SAMPLE_GROUPS, N_CTX_GROUPS).reshape(n_ctx, SEG, D_MODEL)

    st = jnp.stack([_ctx_only(st_f), _ctx_only(st_b)], axis=1)
    new_c = st[..., :A_V_DIM][:, None]
    new_n = st[..., A_V_DIM][:, None]
    new_m = jnp.stack([_ctx_only(m_f)[:, 0, :A_HEADS], _ctx_only(m_b)[:, 0, :A_HEADS]], axis=1)[:, None]
    new_gk = kb[N_SAMPLE_GROUPS:].reshape(n_ctx, 1, SEG, B_KV_HEADS, B_HEAD_DIM)
    new_gv = vb[N_SAMPLE_GROUPS:].reshape(n_ctx, 1, SEG, B_KV_HEADS, B_HEAD_DIM)
    new_ckv = ckv[N_SAMPLE_GROUPS:].reshape(n_ctx, 1, SEG, C_KV_LORA)
    new_kr = kr[N_SAMPLE_GROUPS:, :, :C_ROPE].reshape(n_ctx, 1, SEG, C_ROPE)
    new_rh = jnp.stack([_ctx_only(rs_f)[:, 0], _ctx_only(rs_b)[:, 0]], axis=1)[:, None]
    return (y_prompt, y_sample, new_c, new_n, new_m, new_gk, new_gv, new_ckv, new_kr, new_rh)
```
